```python
import jax, jax.numpy as jnp
from jax import lax
import numpy as np

D_MODEL = 2048
BATCH = 4
SEQ = 2048
DEPTH = 2
DEC_BATCH = 8
DEC_SEQ = 4
PAST_LEN = 16384
PAGE_SIZE = 128

BR_W = D_MODEL // 2
N_BRANCH = 4
A_W = BR_W
A_CONV = 3
HEAD_DIM = 128
N_HEADS = BR_W // HEAD_DIM
GROUPS = ((128, 1), (512, 4), (2048, 16))
N_GROUPS = len(GROUPS)
WINDOW_MAX = 2048
ROT_DIM = HEAD_DIM // 4
ROPE_THETA = 500000.0
Q_BLOCK = 128
C_W = BR_W
POOL_WINDOWS = (2, 4, 8, 16)
C_GROUP = C_W // len(POOL_WINDOWS)
POOL_PAST = POOL_WINDOWS[-1] - 1
D_W = BR_W
D_CONV = 31
EPS = 1e-6
IN_SIZES = (A_W, A_W, A_W, A_W,
            N_GROUPS * N_HEADS * HEAD_DIM, N_HEADS * HEAD_DIM, N_HEADS * HEAD_DIM, BR_W,
            C_W, C_W,
            2 * D_W, D_W,
            N_BRANCH * D_MODEL)
N_IN = sum(IN_SIZES)

kernel_name = 'gated_parallel_conv_dilatedattn_pool_conformer_decoder'


def in_split_points():
    pts, acc = [], 0
    for s in IN_SIZES[:-1]:
        acc += s
        pts.append(acc)
    return pts


def rms_norm(x, g):
    xf = x.astype(jnp.float32)
    y = xf * lax.rsqrt(jnp.mean(xf * xf, axis=-1, keepdims=True) + EPS)
    return (y * g.astype(jnp.float32)).astype(x.dtype)


def layer_norm(x, g, b):
    xf = x.astype(jnp.float32)
    mu = jnp.mean(xf, axis=-1, keepdims=True)
    xc = xf - mu
    var = jnp.mean(xc * xc, axis=-1, keepdims=True)
    y = xc * lax.rsqrt(var + EPS) * g.astype(jnp.float32) + b.astype(jnp.float32)
    return y.astype(x.dtype)


def rope_partial(x, pos):
    half = ROT_DIM // 2
    inv = ROPE_THETA ** (-(jnp.arange(half, dtype=jnp.float32) / half))
    ang = pos.astype(jnp.float32)[:, None] * inv[None, :]
    shape = (1, pos.shape[0]) + (1,) * (x.ndim - 3) + (half,)
    cos = jnp.cos(ang).reshape(shape)
    sin = jnp.sin(ang).reshape(shape)
    xr = x[..., :ROT_DIM].astype(jnp.float32)
    x1, x2 = xr[..., :half], xr[..., half:]
    rot = jnp.concatenate([x1 * cos - x2 * sin, x2 * cos + x1 * sin], axis=-1).astype(x.dtype)
    return jnp.concatenate([rot, x[..., ROT_DIM:]], axis=-1)


def causal_dwconv(u_ext, w):
    c = u_ext.shape[-1]
    return lax.conv_general_dilated(u_ext, w[:, None, :].astype(u_ext.dtype), window_strides=(1,),
                                    padding='VALID', dimension_numbers=('NWC', 'WIO', 'NWC'),
                                    feature_group_count=c)


def pool_mix(u_ext, pos, w_pool, scale):
    t_len = pos.shape[0]
    bsz = u_ext.shape[0]
    L = POOL_WINDOWS[-1]
    uf = u_ext.astype(jnp.float32)
    cs = jnp.concatenate([jnp.zeros_like(uf[:, :1]), jnp.cumsum(uf, axis=1)], axis=1)
    u_new = uf[:, POOL_PAST:]
    outs = []
    for g, w in enumerate(POOL_WINDOWS):
        sl = slice(g * C_GROUP, (g + 1) * C_GROUP)
        s = cs[:, L:, sl] - cs[:, L - w:L - w + t_len, sl]
        cnt = jnp.minimum(pos + 1, w).astype(jnp.float32)[None, :, None]
        outs.append(s / cnt - u_new[:, :, sl])
    p = jnp.stack(outs, axis=2)
    y = jnp.einsum('btgc,gcd->btgd', p, w_pool.astype(jnp.float32)).reshape(bsz, t_len, C_W)
    return (y * scale.astype(jnp.float32)).astype(u_ext.dtype)


def merge_groups(outs, lses):
    o = jnp.stack(outs, axis=0)
    a = jax.nn.softmax(jnp.stack(lses, axis=0), axis=0)
    return jnp.sum(a[..., None] * o, axis=0)


def dilated_block(qb, k_pad, v_pad, t0, w, d, pad):
    bsz = qb.shape[0]
    nj = Q_BLOCK // d
    nk = w // d + 1
    n_l = nk - 1 + nj
    qr = qb.reshape(bsz, nj, d, N_HEADS, HEAD_DIM).astype(jnp.float32)
    r = jnp.arange(d)
    i = jnp.arange(n_l)
    j = jnp.arange(nj)
    pos_k = t0 + r[:, None] + (i[None, :] - (nk - 1)) * d
    idx = pos_k + pad
    kg = k_pad[:, idx].astype(jnp.float32)
    vg = v_pad[:, idx].astype(jnp.float32)
    s = jnp.einsum('bjrhc,brihc->bhrji', qr, kg) * (HEAD_DIM ** -0.5)
    rel = i[None, :] - j[:, None]
    valid = ((rel >= 0) & (rel <= nk - 1))[None, :, :] & (pos_k >= 0)[:, None, :]
    s = jnp.where(valid[None, None], s, -jnp.inf)
    m = jnp.max(s, axis=-1, keepdims=True)
    p = jnp.exp(s - m)
    den = jnp.sum(p, axis=-1, keepdims=True)
    o = jnp.einsum('bhrji,brihc->bjrhc', p / den, vg).reshape(bsz, Q_BLOCK, N_HEADS, HEAD_DIM)
    lse = (m + jnp.log(den))[..., 0].transpose(0, 3, 2, 1).reshape(bsz, Q_BLOCK, N_HEADS)
    return o, lse


def dilated_attn_prompt(q, k, v):
    bsz, s_len = q.shape[:2]
    pad = WINDOW_MAX
    k_pad = jnp.pad(k, ((0, 0), (pad, 0), (0, 0), (0, 0)))
    v_pad = jnp.pad(v, ((0, 0), (pad, 0), (0, 0), (0, 0)))
    n_blk = s_len // Q_BLOCK
    q_blocks = q.reshape(bsz, n_blk, Q_BLOCK, N_GROUPS, N_HEADS, HEAD_DIM).transpose(1, 0, 2, 3, 4, 5)

    def one_block(args):
        qb, blk = args
        t0 = blk * Q_BLOCK
        outs, lses = [], []
        for g, (w, d) in enumerate(GROUPS):
            o, lse = dilated_block(qb[:, :, g], k_pad, v_pad, t0, w, d, pad)
            outs.append(o)
            lses.append(lse)
        return merge_groups(outs, lses).astype(q.dtype)

    o = lax.map(one_block, (q_blocks, jnp.arange(n_blk, dtype=jnp.int32)))
    return o.transpose(1, 0, 2, 3, 4).reshape(bsz, s_len, N_HEADS, HEAD_DIM)


def dilated_attn_sample(q, k_ext, v_ext):
    t_len = q.shape[1]
    n_rows = k_ext.shape[1] - t_len
    s_idx = jnp.arange(t_len)
    outs, lses = [], []
    for g, (w, d) in enumerate(GROUPS):
        nk = w // d + 1
        idx = n_rows + s_idx[:, None] - jnp.arange(nk)[None, :] * d
        valid = idx >= 0
        safe = jnp.maximum(idx, 0)
        kg = k_ext[:, safe].astype(jnp.float32)
        vg = v_ext[:, safe].astype(jnp.float32)
        s = jnp.einsum('bthc,btkhc->bhtk', q[:, :, g].astype(jnp.float32), kg) * (HEAD_DIM ** -0.5)
        s = jnp.where(valid[None, None], s, -jnp.inf)
        m = jnp.max(s, axis=-1, keepdims=True)
        p = jnp.exp(s - m)
        den = jnp.sum(p, axis=-1, keepdims=True)
        outs.append(jnp.einsum('bhtk,btkhc->bthc', p / den, vg))
        lses.append((m + jnp.log(den))[..., 0].transpose(0, 2, 1))
    return merge_groups(outs, lses).astype(q.dtype)


def mixer_layer(x, pos, past, wl):
    bsz, t_len, _ = x.shape
    xn = rms_norm(x, wl['norm_g'])
    h = jnp.einsum('btd,de->bte', xn, wl['w_in'])
    (va, ca, ba, za, q, k, v, zb, uc, zc, glu, zd, gt) = jnp.split(h, in_split_points(), axis=-1)

    ua = ca * va
    pa = jnp.zeros((bsz, A_CONV - 1, A_W), x.dtype) if past is None else past[2].astype(x.dtype)
    ua_ext = jnp.concatenate([pa, ua], axis=1)
    ya = ba * causal_dwconv(ua_ext, wl['a_conv_w']) * jax.nn.silu(za)
    new_a = ua_ext[:, -(A_CONV - 1):]

    q = rms_norm(q.reshape(bsz, t_len, N_GROUPS, N_HEADS, HEAD_DIM), wl['q_norm_g'])
    k = rms_norm(k.reshape(bsz, t_len, N_HEADS, HEAD_DIM), wl['k_norm_g'])
    v = v.reshape(bsz, t_len, N_HEADS, HEAD_DIM)
    q = rope_partial(q, pos)
    k = rope_partial(k, pos)
    if past is None:
        o = dilated_attn_prompt(q, k, v)
        rows = min(WINDOW_MAX, t_len)
        new_k, new_v = k[:, t_len - rows:], v[:, t_len - rows:]
    else:
        rows = past[0].shape[1]
        k_ext = jnp.concatenate([past[0].astype(x.dtype), k], axis=1)
        v_ext = jnp.concatenate([past[1].astype(x.dtype), v], axis=1)
        o = dilated_attn_sample(q, k_ext, v_ext)
        new_k, new_v = k_ext[:, -rows:], v_ext[:, -rows:]
    yb = o.reshape(bsz, t_len, BR_W) * jax.nn.silu(zb)

    pc = jnp.zeros((bsz, POOL_PAST, C_W), x.dtype) if past is None else past[3].astype(x.dtype)
    uc_ext = jnp.concatenate([pc, uc], axis=1)
    yc = pool_mix(uc_ext, pos, wl['c_pool_w'], wl['c_scale']) * jax.nn.silu(zc)
    new_c = uc_ext[:, -POOL_PAST:]

    ga, gb = jnp.split(glu, 2, axis=-1)
    ud = ga * jax.nn.sigmoid(gb)
    pd = jnp.zeros((bsz, D_CONV - 1, D_W), x.dtype) if past is None else past[4].astype(x.dtype)
    ud_ext = jnp.concatenate([pd, ud], axis=1)
    yd = causal_dwconv(ud_ext, wl['d_conv_w']) + wl['d_conv_b']
    yd = jax.nn.silu(layer_norm(yd, wl['d_ln_g'], wl['d_ln_b'])) * jax.nn.silu(zd)
    new_d = ud_ext[:, -(D_CONV - 1):]

    g = jax.nn.sigmoid(gt.reshape(bsz, t_len, N_BRANCH, D_MODEL))
    merged = (g[:, :, 0] * jnp.einsum('btc,cd->btd', ya, wl['w_br_a'])
              + g[:, :, 1] * jnp.einsum('btc,cd->btd', yb, wl['w_br_b'])
              + g[:, :, 2] * jnp.einsum('btc,cd->btd', yc, wl['w_br_c'])
              + g[:, :, 3] * jnp.einsum('btc,cd->btd', yd, wl['w_br_d']))
    out = x + jnp.einsum('btd,de->bte', merged, wl['w_out'])
    return out, (new_k, new_v, new_a, new_c, new_d)


def setup_inputs(seed: int = 0) -> dict:
    key = jax.random.key(seed)
    ks = jax.random.split(key, 24)

    def nrm(k, shape, scale):
        return jax.random.normal(k, shape, jnp.float32) * scale

    wbuf = min(WINDOW_MAX, PAST_LEN)
    return {
        'x_prompt': nrm(ks[0], (BATCH, SEQ, D_MODEL), 1.0),
        'x_sample': nrm(ks[1], (DEC_BATCH, DEC_SEQ, D_MODEL), 1.0),
        'cache_attn_k': nrm(ks[2], (DEPTH, DEC_BATCH, wbuf, N_HEADS, HEAD_DIM), 1.0),
        'cache_attn_v': nrm(ks[3], (DEPTH, DEC_BATCH, wbuf, N_HEADS, HEAD_DIM), 1.0),
        'state_conv_a': nrm(ks[4], (DEPTH, DEC_BATCH, A_CONV - 1, A_W), 1.0),
        'state_pool_c': nrm(ks[5], (DEPTH, DEC_BATCH, POOL_PAST, C_W), 1.0),
        'state_conv_d': nrm(ks[6], (DEPTH, DEC_BATCH, D_CONV - 1, D_W), 0.5),
        'norm_g': 1.0 + nrm(ks[7], (DEPTH, D_MODEL), 0.05),
        'w_in': nrm(ks[8], (DEPTH, D_MODEL, N_IN), D_MODEL ** -0.5),
        'q_norm_g': 1.0 + nrm(ks[9], (DEPTH, HEAD_DIM), 0.05),
        'k_norm_g': 1.0 + nrm(ks[10], (DEPTH, HEAD_DIM), 0.05),
        'a_conv_w': nrm(ks[11], (DEPTH, A_CONV, A_W), A_CONV ** -0.5),
        'c_pool_w': nrm(ks[12], (DEPTH, len(POOL_WINDOWS), C_GROUP, C_GROUP), C_GROUP ** -0.5),
        'c_scale': 1.0 + nrm(ks[13], (DEPTH, C_W), 0.05),
        'd_conv_w': nrm(ks[14], (DEPTH, D_CONV, D_W), D_CONV ** -0.5),
        'd_conv_b': nrm(ks[15], (DEPTH, D_W), 0.01),
        'd_ln_g': 1.0 + nrm(ks[16], (DEPTH, D_W), 0.05),
        'd_ln_b': nrm(ks[17], (DEPTH, D_W), 0.01),
        'w_br_a': nrm(ks[18], (DEPTH, A_W, D_MODEL), A_W ** -0.5),
        'w_br_b': nrm(ks[19], (DEPTH, BR_W, D_MODEL), BR_W ** -0.5),
        'w_br_c': nrm(ks[20], (DEPTH, C_W, D_MODEL), C_W ** -0.5),
        'w_br_d': nrm(ks[21], (DEPTH, D_W, D_MODEL), D_W ** -0.5),
        'w_out': nrm(ks[22], (DEPTH, D_MODEL, D_MODEL), D_MODEL ** -0.5),
    }


def reference(x_prompt, x_sample, cache_attn_k, cache_attn_v, state_conv_a, state_pool_c, state_conv_d,
              norm_g, w_in, q_norm_g, k_norm_g, a_conv_w, c_pool_w, c_scale, d_conv_w, d_conv_b,
              d_ln_g, d_ln_b, w_br_a, w_br_b, w_br_c, w_br_d, w_out):
    pos_p = jnp.arange(x_prompt.shape[1], dtype=jnp.int32)
    pos_s = PAST_LEN + jnp.arange(x_sample.shape[1], dtype=jnp.int32)
    hp, hs = x_prompt, x_sample
    st_p = [[], [], [], [], []]
    st_s = [[], [], [], [], []]
    for l in range(DEPTH):
        wl = {'norm_g': norm_g[l], 'w_in': w_in[l], 'q_norm_g': q_norm_g[l], 'k_norm_g': k_norm_g[l],
              'a_conv_w': a_conv_w[l], 'c_pool_w': c_pool_w[l], 'c_scale': c_scale[l],
              'd_conv_w': d_conv_w[l], 'd_conv_b': d_conv_b[l], 'd_ln_g': d_ln_g[l], 'd_ln_b': d_ln_b[l],
              'w_br_a': w_br_a[l], 'w_br_b': w_br_b[l], 'w_br_c': w_br_c[l], 'w_br_d': w_br_d[l],
              'w_out': w_out[l]}
        hp, sp = mixer_layer(hp, pos_p, None, wl)
        hs, ss = mixer_layer(hs, pos_s, (cache_attn_k[l], cache_attn_v[l], state_conv_a[l],
                                         state_pool_c[l], state_conv_d[l]), wl)
        for i in range(5):
            st_p[i].append(sp[i])
            st_s[i].append(ss[i])
    new_k_prompt = jnp.stack(st_p[0], axis=0)
    new_v_prompt = jnp.stack(st_p[1], axis=0)
    new_conv_a_prompt = jnp.stack(st_p[2], axis=0)
    new_pool_c_prompt = jnp.stack(st_p[3], axis=0)
    new_conv_d_prompt = jnp.stack(st_p[4], axis=0)
    new_k_sample = jnp.stack(st_s[0], axis=0)
    new_v_sample = jnp.stack(st_s[1], axis=0)
    new_conv_a_sample = jnp.stack(st_s[2], axis=0)
    new_pool_c_sample = jnp.stack(st_s[3], axis=0)
    new_conv_d_sample = jnp.stack(st_s[4], axis=0)
    return (hp, hs, new_k_prompt, new_v_prompt, new_conv_a_prompt, new_pool_c_prompt, new_conv_d_prompt,
            new_k_sample, new_v_sample, new_conv_a_sample, new_pool_c_sample, new_conv_d_sample)
```

```python
import functools

import jax
import jax.numpy as jnp
from jax import lax
from jax.experimental import pallas as pl
from jax.experimental.pallas import tpu as pltpu

D_MODEL = 2048
DEPTH = 2
PAST_LEN = 16384
BR_W = D_MODEL // 2
N_BRANCH = 4
A_CONV = 3
HEAD_DIM = 128
N_HEADS = BR_W // HEAD_DIM
GROUPS = ((128, 1), (512, 4), (2048, 16))
N_GROUPS = len(GROUPS)
ROT_DIM = HEAD_DIM // 4
ROPE_THETA = 500000.0
POOL_WINDOWS = (2, 4, 8, 16)
C_GROUP = BR_W // len(POOL_WINDOWS)
POOL_PAST = POOL_WINDOWS[-1] - 1
D_CONV = 31
EPS = 1e-6
N_IN = 4 * BR_W + (N_GROUPS + 3) * BR_W + 2 * BR_W + 3 * BR_W + N_BRANCH * D_MODEL

COL_VA, COL_CA, COL_BA, COL_ZA = 0, 1, 2, 3
COL_Q, COL_K, COL_V, COL_ZB = 4, 7, 8, 9
COL_UC, COL_ZC = 10, 11
COL_GA, COL_GB, COL_ZD = 12, 13, 14
COL_GATE = 15

LANES = 128
QB = 128
HALO = 32
NEG = -1e30
VMEM_LIMIT = 56 * 1024 * 1024

_BF16 = jnp.bfloat16
_F32 = jnp.float32


def _silu(x):
    return x * jax.nn.sigmoid(x)


def _params(sem):
    return pltpu.CompilerParams(dimension_semantics=sem, vmem_limit_bytes=VMEM_LIMIT)


def _inproj_kernel(x_ref, g_ref, w_ref, h_ref, xn_ref):
    @pl.when(pl.program_id(1) == 0)
    def _():
        x = x_ref[...]
        y = x * lax.rsqrt(jnp.mean(x * x, axis=-1, keepdims=True) + EPS)
        xn_ref[...] = (y * g_ref[...]).astype(_BF16)

    h_ref[...] = jnp.dot(xn_ref[...], w_ref[...], preferred_element_type=_F32)


def _inproj(x2, g, w_in, layer, tm, tn=1024):
    m = x2.shape[0]
    return pl.pallas_call(
        _inproj_kernel,
        grid=(m // tm, N_IN // tn),
        in_specs=[
            pl.BlockSpec((tm, D_MODEL), lambda i, j: (i, 0)),
            pl.BlockSpec((1, D_MODEL), lambda i, j: (0, 0)),
            pl.BlockSpec((None, D_MODEL, tn), lambda i, j: (layer, 0, j)),
        ],
        out_specs=pl.BlockSpec((tm, tn), lambda i, j: (i, j)),
        out_shape=jax.ShapeDtypeStruct((m, N_IN), _F32),
        scratch_shapes=[pltpu.VMEM((tm, D_MODEL), _BF16)],
        compiler_params=_params(("parallel", "arbitrary")),
        name="inproj",
    )(x2, g, w_in)


def _rope_tables(pos, lane, inv):
    ang = pos * inv
    c = jnp.cos(ang)
    s = jnp.sin(ang)
    half = ROT_DIM // 2
    cos_t = jnp.where(lane < ROT_DIM, c, 1.0)
    sin_a = jnp.where(lane < half, -s, 0.0)
    sin_b = jnp.where(lane < half, 0.0, jnp.where(lane < ROT_DIM, s, 0.0))
    return cos_t, sin_a, sin_b


def _rope(x, cos_t, sin_a, sin_b):
    ax = x.ndim - 1
    return x * cos_t + pltpu.roll(x, LANES - ROT_DIM // 2, ax) * sin_a + pltpu.roll(x, ROT_DIM // 2, ax) * sin_b


def _rms_heads(x, g):
    return x * lax.rsqrt(jnp.mean(x * x, axis=-1, keepdims=True) + EPS) * g


def _attn_block(q, k, v, mask):
    s = lax.dot_general(q.astype(_BF16), k.astype(_BF16), (((1,), (1,)), ((), ())),
                        preferred_element_type=_F32)
    s = jnp.where(mask, s, NEG)
    m = jnp.max(s, axis=-1, keepdims=True)
    p = jnp.exp(s - m)
    l = jnp.sum(p, axis=-1, keepdims=True)
    acc = jnp.dot(p.astype(_BF16), v.astype(_BF16), preferred_element_type=_F32)
    return acc / l, m + jnp.log(l)


def _attn_prompt_kernel(q0_ref, q1_ref, q2_ref, k_ref, v_ref, zb_ref, qg_ref, kg_ref, inv_ref,
                        yb_ref, ko_ref, vo_ref,
                        cos_ref, sa_ref, sb_ref, qs_ref, og_ref, lg_ref, *, seq):
    rows = 256
    n_chunks = seq // rows

    @pl.when((pl.program_id(0) == 0) & (pl.program_id(1) == 0))
    def _():
        def body(c, carry):
            r0 = pl.multiple_of(c * rows, rows)
            pos = (lax.broadcasted_iota(jnp.int32, (rows, LANES), 0) + r0).astype(_F32)
            lane = lax.broadcasted_iota(jnp.int32, (rows, LANES), 1)
            cos_t, sin_a, sin_b = _rope_tables(pos, lane, inv_ref[...])
            cos_ref[pl.ds(r0, rows), :] = cos_t
            sa_ref[pl.ds(r0, rows), :] = sin_a
            sb_ref[pl.ds(r0, rows), :] = sin_b
            return carry
        lax.fori_loop(0, n_chunks, body, 0)

    scale = HEAD_DIM ** -0.5

    def prep(c, carry):
        r0 = pl.multiple_of(c * rows, rows)
        sl = pl.ds(r0, rows)
        cos_t, sin_a, sin_b = cos_ref[sl, :], sa_ref[sl, :], sb_ref[sl, :]
        for g, q_ref in enumerate((q0_ref, q1_ref, q2_ref)):
            q = _rope(_rms_heads(q_ref[0, sl, :], qg_ref[...]), cos_t, sin_a, sin_b)
            qs_ref[g, sl, :] = q * scale
        ko_ref[0, sl, :] = _rope(_rms_heads(k_ref[0, sl, :], kg_ref[...]), cos_t, sin_a, sin_b)
        vo_ref[0, sl, :] = v_ref[0, sl, :]
        return carry
    lax.fori_loop(0, n_chunks, prep, 0)

    ri = lax.broadcasted_iota(jnp.int32, (QB, 2 * QB), 0)
    ci = lax.broadcasted_iota(jnp.int32, (QB, 2 * QB), 1)
    band = (ci >= ri) & (ci <= ri + QB)
    causal = lax.broadcasted_iota(jnp.int32, (QB, QB), 1) <= lax.broadcasted_iota(jnp.int32, (QB, QB), 0)

    def rows_of(start, n, d):
        return pl.ds(start, n) if d == 1 else pl.ds(start, n, stride=d)

    for g, (w, d) in enumerate(GROUPS):
        assert w // d == QB
        n_blk = seq // (d * QB)
        for r in range(d):
            for blk in range(n_blk):
                q = qs_ref[g, rows_of(r + d * QB * blk, QB, d), :]
                if blk == 0:
                    ksl, mask = rows_of(r, QB, d), causal
                else:
                    ksl, mask = rows_of(r + d * QB * (blk - 1), 2 * QB, d), band
                o, lse = _attn_block(q, ko_ref[0, ksl, :], v_ref[0, ksl, :], mask)
                osl = rows_of(r + d * QB * blk, QB, d)
                og_ref[g, osl, :] = o
                lg_ref[g, osl, :] = jnp.broadcast_to(lse, (QB, LANES))

    def merge(c, carry):
        r0 = pl.multiple_of(c * rows, rows)
        sl = pl.ds(r0, rows)
        l0, l1, l2 = lg_ref[0, sl, :], lg_ref[1, sl, :], lg_ref[2, sl, :]
        mx = jnp.maximum(jnp.maximum(l0, l1), l2)
        w0, w1, w2 = jnp.exp(l0 - mx), jnp.exp(l1 - mx), jnp.exp(l2 - mx)
        o = (w0 * og_ref[0, sl, :] + w1 * og_ref[1, sl, :] + w2 * og_ref[2, sl, :]) / (w0 + w1 + w2)
        yb_ref[0, sl, :] = (o * _silu(zb_ref[0, sl, :])).astype(_BF16)
        return carry
    lax.fori_loop(0, n_chunks, merge, 0)


def _attn_prompt(h3, qg, kg, inv):
    b, seq, _ = h3.shape
    hb = BR_W // LANES

    def col(cb):
        return pl.BlockSpec((1, seq, LANES), lambda i, j: (i, 0, cb * hb + j))

    small = pl.BlockSpec((1, LANES), lambda i, j: (0, 0))
    out = pl.BlockSpec((1, seq, LANES), lambda i, j: (i, 0, j))
    return pl.pallas_call(
        functools.partial(_attn_prompt_kernel, seq=seq),
        grid=(b, N_HEADS),
        in_specs=[col(COL_Q), col(COL_Q + 1), col(COL_Q + 2), col(COL_K), col(COL_V), col(COL_ZB),
                  small, small, small],
        out_specs=[out, out, out],
        out_shape=[jax.ShapeDtypeStruct((b, seq, BR_W), _BF16),
                   jax.ShapeDtypeStruct((b, seq, BR_W), _F32),
                   jax.ShapeDtypeStruct((b, seq, BR_W), _F32)],
        scratch_shapes=[pltpu.VMEM((seq, LANES), _F32)] * 3
        + [pltpu.VMEM((N_GROUPS, seq, LANES), _F32)] * 3,
        compiler_params=_params(("arbitrary", "arbitrary")),
        name="attn_prompt",
    )(h3, h3, h3, h3, h3, h3, qg, kg, inv)


def _attn_sample_kernel(q0_ref, q1_ref, q2_ref, k_ref, v_ref, zb_ref, qg_ref, kg_ref, inv_ref,
                        kc_ref, vc_ref, yb_ref, kn_ref, vn_ref, *, t_len, n_rows):
    shape = (t_len, N_HEADS, LANES)
    pos = (lax.broadcasted_iota(jnp.int32, shape, 0) + PAST_LEN).astype(_F32)
    lane = lax.broadcasted_iota(jnp.int32, shape, 2)
    cos_t, sin_a, sin_b = _rope_tables(pos, lane, inv_ref[...].reshape(1, 1, LANES))
    qg = qg_ref[...].reshape(1, 1, LANES)
    kg = kg_ref[...].reshape(1, 1, LANES)
    scale = HEAD_DIM ** -0.5
    k_new = _rope(_rms_heads(k_ref[0], kg), cos_t, sin_a, sin_b)
    v_new = v_ref[0]
    kn_ref[0] = k_new
    vn_ref[0] = v_new
    nj = n_rows // 16
    qs = [_rope(_rms_heads(q_ref[0], qg), cos_t, sin_a, sin_b) * scale for q_ref in (q0_ref, q1_ref, q2_ref)]

    for t in range(t_len):
        outs, lses = [], []
        for g, (w, d) in enumerate(GROUPS):
            q = qs[g][t]
            pieces = []
            if d == 1:
                lo = nj - QB // 16
                kc = kc_ref[0, 0, lo:nj].reshape(QB, N_HEADS, LANES)
                vc = vc_ref[0, 0, lo:nj].reshape(QB, N_HEADS, LANES)
                idx = lax.broadcasted_iota(jnp.int32, (QB, N_HEADS, 1), 0)
                pieces.append((kc, vc, idx >= t))
                pieces.append((k_new[:t + 1], v_new[:t + 1], None))
            else:
                per = 16 // d
                lo = nj - (w // d) // per
                for m in range(per):
                    pieces.append((kc_ref[0, 0, lo:nj, t + d * m], vc_ref[0, 0, lo:nj, t + d * m], None))
                pieces.append((k_new[t:t + 1], v_new[t:t + 1], None))
            scores = []
            for kk, vv, mask in pieces:
                s = jnp.sum(kk * q[None], axis=-1, keepdims=True)
                if mask is not None:
                    s = jnp.where(mask, s, NEG)
                scores.append(s)
            mx = functools.reduce(jnp.maximum, [jnp.max(s, axis=0) for s in scores])
            den = 0.0
            acc = 0.0
            for s, (kk, vv, mask) in zip(scores, pieces):
                p = jnp.exp(s - mx[None])
                den = den + jnp.sum(p, axis=0)
                acc = acc + jnp.sum(p * vv, axis=0)
            outs.append(acc / den)
            lses.append(mx + jnp.log(den))
        mx = jnp.maximum(jnp.maximum(lses[0], lses[1]), lses[2])
        ws = [jnp.exp(l - mx) for l in lses]
        o = (ws[0] * outs[0] + ws[1] * outs[1] + ws[2] * outs[2]) / (ws[0] + ws[1] + ws[2])
        yb_ref[0, t] = o * _silu(zb_ref[0, t])


def _attn_sample(h4, qg, kg, inv, cache_k, cache_v, layer):
    b, t_len = h4.shape[:2]
    n_rows = cache_k.shape[2]
    kc = cache_k.reshape(DEPTH, b, n_rows // 16, 16, N_HEADS, LANES)
    vc = cache_v.reshape(DEPTH, b, n_rows // 16, 16, N_HEADS, LANES)

    def col(cb):
        return pl.BlockSpec((1, t_len, N_HEADS, LANES), lambda i: (i, 0, cb, 0))

    small = pl.BlockSpec((1, LANES), lambda i: (0, 0))
    cache = pl.BlockSpec((1, 1, n_rows // 16, 16, N_HEADS, LANES), lambda i: (layer, i, 0, 0, 0, 0))
    out = pl.BlockSpec((1, t_len, N_HEADS, LANES), lambda i: (i, 0, 0, 0))
    return pl.pallas_call(
        functools.partial(_attn_sample_kernel, t_len=t_len, n_rows=n_rows),
        grid=(b,),
        in_specs=[col(COL_Q), col(COL_Q + 1), col(COL_Q + 2), col(COL_K), col(COL_V), col(COL_ZB),
                  small, small, small, cache, cache],
        out_specs=[out, out, out],
        out_shape=[jax.ShapeDtypeStruct((b, t_len, N_HEADS, LANES), _F32)] * 3,
        compiler_params=_params(("arbitrary",)),
        name="attn_sample",
    )(h4, h4, h4, h4, h4, h4, qg, kg, inv, kc, vc)


def _branch_kernel(va_ref, ca_ref, ba_ref, za_ref, uc_ref, zc_ref, ga_ref, gb_ref, zd_ref,
                   sta_ref, stc_ref, std_ref, aw_ref, pw_ref, cs_ref, dw_ref, db_ref, lg_ref, lb_ref,
                   ya_ref, yc_ref, yd_ref, na_ref, nc_ref, nd_ref,
                   ea_ref, ec_ref, ed_ref, dacc_ref, pp_ref, *, tm, pos0):
    i = pl.program_id(1)

    @pl.when(i == 0)
    def _():
        ea_ref[0:HALO, :] = sta_ref[0]
        ec_ref[0:HALO, :] = stc_ref[0]
        ed_ref[0:HALO, :] = std_ref[0]
        pp_ref[...] = jnp.zeros_like(pp_ref)

    @pl.when(i > 0)
    def _():
        for e_ref in (ea_ref, ec_ref, ed_ref):
            e_ref[0:HALO, :] = e_ref[tm:tm + HALO, :]

    new = pl.ds(HALO, tm)
    ea_ref[new, :] = ca_ref[0] * va_ref[0]
    ec_ref[new, :] = uc_ref[0]
    ed_ref[new, :] = ga_ref[0] * jax.nn.sigmoid(gb_ref[0])

    conv = aw_ref[0:1, :] * ea_ref[pl.ds(HALO - 2, tm), :]
    for k in range(1, A_CONV):
        conv = conv + aw_ref[k:k + 1, :] * ea_ref[pl.ds(HALO - 2 + k, tm), :]
    ya_ref[0] = (ba_ref[0] * conv * _silu(za_ref[0])).astype(_BF16)

    pos = pos0 + i * tm + lax.broadcasted_iota(jnp.int32, (tm, C_GROUP), 0)
    for g, w in enumerate(POOL_WINDOWS):
        cols = slice(g * C_GROUP, (g + 1) * C_GROUP)
        tok = ec_ref[new, cols]
        s = tok
        for j in range(1, w):
            s = s + ec_ref[pl.ds(HALO - j, tm), cols]
        cnt = jnp.minimum(pos + 1, w).astype(_F32)
        pp_ref[0:tm, :] = s / cnt - tok
        y = jnp.dot(pp_ref[...].astype(_BF16), pw_ref[g], preferred_element_type=_F32)[0:tm]
        yc_ref[0, :, cols] = ((y * cs_ref[:, cols]) * _silu(zc_ref[0, :, cols])).astype(_BF16)

    rc = min(tm, 128)
    for c in range(BR_W // LANES):
        cols = slice(c * LANES, (c + 1) * LANES)
        for r0 in range(0, tm, rc):
            acc = dw_ref[0:1, cols] * ed_ref[pl.ds(HALO - (D_CONV - 1) + r0, rc), cols]
            for k in range(1, D_CONV):
                acc = acc + dw_ref[k:k + 1, cols] * ed_ref[pl.ds(HALO - (D_CONV - 1) + k + r0, rc), cols]
            dacc_ref[pl.ds(r0, rc), cols] = acc + db_ref[:, cols]
    x = dacc_ref[0:tm, :]
    mu = jnp.mean(x, axis=-1, keepdims=True)
    xc = x - mu
    var = jnp.mean(xc * xc, axis=-1, keepdims=True)
    y = xc * lax.rsqrt(var + EPS) * lg_ref[...] + lb_ref[...]
    yd_ref[0] = (_silu(y) * _silu(zd_ref[0])).astype(_BF16)

    @pl.when(i == pl.num_programs(1) - 1)
    def _():
        na_ref[0] = ea_ref[pl.ds(HALO + tm - (A_CONV - 1), A_CONV - 1), :]
        nc_ref[0] = ec_ref[pl.ds(HALO + tm - POOL_PAST, POOL_PAST), :]
        nd_ref[0] = ed_ref[pl.ds(HALO + tm - (D_CONV - 1), D_CONV - 1), :]


def _branches(h3, states, wl, pos0, tm):
    b, t_len, _ = h3.shape

    def col(cb):
        return pl.BlockSpec((1, tm, BR_W), lambda i, j: (i, j, cb))

    def full(shape):
        return pl.BlockSpec(shape, lambda i, j: (0,) * len(shape))

    state = pl.BlockSpec((1, HALO, BR_W), lambda i, j: (i, 0, 0))
    y_out = pl.BlockSpec((1, tm, BR_W), lambda i, j: (i, j, 0))

    def st_out(n):
        return pl.BlockSpec((1, n, BR_W), lambda i, j: (i, 0, 0))

    tmc = max(tm, 8)
    n_pool = len(POOL_WINDOWS)
    return pl.pallas_call(
        functools.partial(_branch_kernel, tm=tm, pos0=pos0),
        grid=(b, t_len // tm),
        in_specs=[col(COL_VA), col(COL_CA), col(COL_BA), col(COL_ZA), col(COL_UC), col(COL_ZC),
                  col(COL_GA), col(COL_GB), col(COL_ZD), state, state, state,
                  full((A_CONV, BR_W)), full((n_pool, C_GROUP, C_GROUP)), full((1, BR_W)),
                  full((D_CONV, BR_W)), full((1, BR_W)), full((1, BR_W)), full((1, BR_W))],
        out_specs=[y_out, y_out, y_out, st_out(A_CONV - 1), st_out(POOL_PAST), st_out(D_CONV - 1)],
        out_shape=[jax.ShapeDtypeStruct((b, t_len, BR_W), _BF16)] * 3
        + [jax.ShapeDtypeStruct((b, n, BR_W), _F32) for n in (A_CONV - 1, POOL_PAST, D_CONV - 1)],
        scratch_shapes=[pltpu.VMEM((HALO + tmc, BR_W), _F32)] * 3
        + [pltpu.VMEM((tmc, BR_W), _F32), pltpu.VMEM((tmc, C_GROUP), _F32)],
        compiler_params=_params(("arbitrary", "arbitrary")),
        name="branches",
    )(*([h3] * 9), *states, wl["a_conv_w"], wl["c_pool_w"], wl["c_scale"], wl["d_conv_w"],
      wl["d_conv_b"], wl["d_ln_g"], wl["d_ln_b"])


def _merge_kernel(ya_ref, yb_ref, yc_ref, yd_ref, wa_ref, wb_ref, wc_ref, wd_ref,
                  g0_ref, g1_ref, g2_ref, g3_ref, o_ref):
    acc = None
    for y_ref, w_ref, g_ref in ((ya_ref, wa_ref, g0_ref), (yb_ref, wb_ref, g1_ref),
                                (yc_ref, wc_ref, g2_ref), (yd_ref, wd_ref, g3_ref)):
        term = jax.nn.sigmoid(g_ref[...]) * jnp.dot(y_ref[...].astype(_BF16), w_ref[...],
                                                    preferred_element_type=_F32)
        acc = term if acc is None else acc + term
    o_ref[...] = acc.astype(_BF16)


def _merge(ys, w_brs, h2, layer, tm, tn=512):
    m = h2.shape[0]
    gate0 = COL_GATE * BR_W // tn
    y_spec = pl.BlockSpec((tm, BR_W), lambda i, j: (i, 0))
    w_spec = pl.BlockSpec((None, BR_W, tn), lambda i, j: (layer, 0, j))

    def gate(n):
        return pl.BlockSpec((tm, tn), lambda i, j: (i, gate0 + n * (D_MODEL // tn) + j))

    return pl.pallas_call(
        _merge_kernel,
        grid=(m // tm, D_MODEL // tn),
        in_specs=[y_spec] * 4 + [w_spec] * 4 + [gate(n) for n in range(N_BRANCH)],
        out_specs=pl.BlockSpec((tm, tn), lambda i, j: (i, j)),
        out_shape=jax.ShapeDtypeStruct((m, D_MODEL), _BF16),
        compiler_params=_params(("parallel", "arbitrary")),
        name="merge",
    )(*ys, *w_brs, h2, h2, h2, h2)


def _outproj_kernel(m_ref, w_ref, x_ref, o_ref):
    o_ref[...] = x_ref[...] + jnp.dot(m_ref[...], w_ref[...], preferred_element_type=_F32)


def _outproj(merged, w_out, x2, layer, tm, tn=1024):
    m = x2.shape[0]
    return pl.pallas_call(
        _outproj_kernel,
        grid=(m // tm, D_MODEL // tn),
        in_specs=[pl.BlockSpec((tm, D_MODEL), lambda i, j: (i, 0)),
                  pl.BlockSpec((None, D_MODEL, tn), lambda i, j: (layer, 0, j)),
                  pl.BlockSpec((tm, tn), lambda i, j: (i, j))],
        out_specs=pl.BlockSpec((tm, tn), lambda i, j: (i, j)),
        out_shape=jax.ShapeDtypeStruct((m, D_MODEL), _F32),
        compiler_params=_params(("parallel", "arbitrary")),
        name="outproj",
    )(merged, w_out, x2)


def _layer(x3, layer, past, wl, big):
    b, t_len, _ = x3.shape
    m = b * t_len
    x2 = x3.reshape(m, D_MODEL)
    tm_mat = min(m, 1024)
    h2 = _inproj(x2, wl["norm_g"], big["w_in"], layer, tm_mat)
    h3 = h2.reshape(b, t_len, N_IN)

    if past is None:
        yb, k_new, v_new = _attn_prompt(h3, wl["q_norm_g"], wl["k_norm_g"], wl["inv"])
        new_k = k_new.reshape(b, t_len, N_HEADS, HEAD_DIM)
        new_v = v_new.reshape(b, t_len, N_HEADS, HEAD_DIM)
        states = [jnp.zeros((b, HALO, BR_W), _F32)] * 3
        pos0, tm_br = 0, 256
    else:
        h4 = h2.reshape(b, t_len, N_IN // LANES, LANES)
        yb, k_new, v_new = _attn_sample(h4, wl["q_norm_g"], wl["k_norm_g"], wl["inv"], past[0], past[1], layer)
        new_k = jnp.concatenate([past[0][layer][:, t_len:], k_new], axis=1)
        new_v = jnp.concatenate([past[1][layer][:, t_len:], v_new], axis=1)
        states = [jnp.pad(s[layer], ((0, 0), (HALO - s.shape[2], 0), (0, 0))) for s in past[2:]]
        pos0, tm_br = PAST_LEN, t_len
    yb = yb.reshape(m, BR_W)

    ya, yc, yd, new_a, new_c, new_d = _branches(h3, states, wl, pos0, tm_br)
    ys = (ya.reshape(m, BR_W), yb, yc.reshape(m, BR_W), yd.reshape(m, BR_W))
    merged = _merge(ys, big["w_br"], h2, layer, tm_mat)
    out = _outproj(merged, big["w_out"], x2, layer, tm_mat)
    return out.reshape(b, t_len, D_MODEL), (new_k, new_v, new_a, new_c, new_d)


def kernel(x_prompt, x_sample, cache_attn_k, cache_attn_v, state_conv_a, state_pool_c, state_conv_d,
           norm_g, w_in, q_norm_g, k_norm_g, a_conv_w, c_pool_w, c_scale, d_conv_w, d_conv_b,
           d_ln_g, d_ln_b, w_br_a, w_br_b, w_br_c, w_br_d, w_out):
    half = ROT_DIM // 2
    inv = ROPE_THETA ** (-(jnp.arange(half, dtype=_F32) / half))
    inv = jnp.concatenate([inv, inv, jnp.zeros((LANES - ROT_DIM,), _F32)])[None, :]
    big = {"w_in": w_in.astype(_BF16), "w_out": w_out.astype(_BF16),
           "w_br": tuple(w.astype(_BF16) for w in (w_br_a, w_br_b, w_br_c, w_br_d))}
    pool_w = c_pool_w.astype(_BF16)

    hp, hs = x_prompt, x_sample
    st_p = [[] for _ in range(5)]
    st_s = [[] for _ in range(5)]
    for l in range(DEPTH):
        wl = {"norm_g": norm_g[l][None], "q_norm_g": q_norm_g[l][None], "k_norm_g": k_norm_g[l][None],
              "inv": inv, "a_conv_w": a_conv_w[l], "c_pool_w": pool_w[l], "c_scale": c_scale[l][None],
              "d_conv_w": d_conv_w[l], "d_conv_b": d_conv_b[l][None], "d_ln_g": d_ln_g[l][None],
              "d_ln_b": d_ln_b[l][None]}
        hp, sp = _layer(hp, l, None, wl, big)
        hs, ss = _layer(hs, l, (cache_attn_k, cache_attn_v, state_conv_a, state_pool_c, state_conv_d), wl, big)
        for n in range(5):
            st_p[n].append(sp[n])
            st_s[n].append(ss[n])
    return (hp, hs, *(jnp.stack(s, axis=0) for s in st_p), *(jnp.stack(s, axis=0) for s in st_s))
```

```python
import functools

import jax
import jax.numpy as jnp
from jax import lax
from jax.experimental import pallas as pl
from jax.experimental.pallas import tpu as pltpu

D_MODEL = 2048
DEPTH = 2
PAST_LEN = 16384
BR_W = D_MODEL // 2
N_BRANCH = 4
A_CONV = 3
HEAD_DIM = 128
N_HEADS = BR_W // HEAD_DIM
GROUPS = ((128, 1), (512, 4), (2048, 16))
N_GROUPS = len(GROUPS)
ROT_DIM = HEAD_DIM // 4
ROPE_THETA = 500000.0
POOL_WINDOWS = (2, 4, 8, 16)
C_GROUP = BR_W // len(POOL_WINDOWS)
POOL_PAST = POOL_WINDOWS[-1] - 1
D_CONV = 31
EPS = 1e-6
N_IN = 4 * BR_W + (N_GROUPS + 3) * BR_W + 2 * BR_W + 3 * BR_W + N_BRANCH * D_MODEL

COL_VA, COL_CA, COL_BA, COL_ZA = 0, 1, 2, 3
COL_Q, COL_K, COL_V, COL_ZB = 4, 7, 8, 9
COL_UC, COL_ZC = 10, 11
COL_GA, COL_GB, COL_ZD = 12, 13, 14
COL_GATE = 15

LANES = 128
QB = 128
HALO = 32
NEG = -1e30
VMEM_LIMIT = 56 * 1024 * 1024

_BF16 = jnp.bfloat16
_F32 = jnp.float32


def _sigmoid(x):
    return 0.5 * jnp.tanh(0.5 * x) + 0.5


def _silu(x):
    return x * _sigmoid(x)


def _params(sem):
    return pltpu.CompilerParams(dimension_semantics=sem, vmem_limit_bytes=VMEM_LIMIT)


def _inproj_kernel(x_ref, g_ref, w_ref, h_ref, xn_ref):
    @pl.when(pl.program_id(1) == 0)
    def _():
        x = x_ref[...]
        y = x * lax.rsqrt(jnp.mean(x * x, axis=-1, keepdims=True) + EPS)
        xn_ref[...] = (y * g_ref[...]).astype(_BF16)

    h_ref[...] = jnp.dot(xn_ref[...], w_ref[...], preferred_element_type=_F32)


def _inproj(x2, g, w_in, layer, tm, tn=1024):
    m = x2.shape[0]
    return pl.pallas_call(
        _inproj_kernel,
        grid=(m // tm, N_IN // tn),
        in_specs=[
            pl.BlockSpec((tm, D_MODEL), lambda i, j: (i, 0)),
            pl.BlockSpec((1, D_MODEL), lambda i, j: (0, 0)),
            pl.BlockSpec((None, D_MODEL, tn), lambda i, j: (layer, 0, j)),
        ],
        out_specs=pl.BlockSpec((tm, tn), lambda i, j: (i, j)),
        out_shape=jax.ShapeDtypeStruct((m, N_IN), _F32),
        scratch_shapes=[pltpu.VMEM((tm, D_MODEL), _BF16)],
        compiler_params=_params(("parallel", "arbitrary")),
        name="inproj",
    )(x2, g, w_in)


def _rope_tables(pos, lane, inv):
    ang = pos * inv
    cos_t = jnp.where(lane < ROT_DIM, jnp.cos(ang), 1.0)
    sin_t = jnp.where(lane < ROT_DIM, jnp.sin(ang), 0.0)
    return cos_t, sin_t


def _rotate_half_matrix():
    half = ROT_DIM // 2
    m = lax.broadcasted_iota(jnp.int32, (LANES, LANES), 0)
    l = lax.broadcasted_iota(jnp.int32, (LANES, LANES), 1)
    return jnp.where(m == l + half, jnp.where(l < half, -1.0, 0.0),
                     jnp.where(m == l - half, jnp.where(l < ROT_DIM, 1.0, 0.0), 0.0))


def _rotate_half_lanes(x):
    ax = x.ndim - 1
    half = ROT_DIM // 2
    lane = lax.broadcasted_iota(jnp.int32, x.shape, ax)
    return jnp.where(lane < half, -pltpu.roll(x, LANES - half, ax), pltpu.roll(x, half, ax))


def _dot_hi_lo(x, w2):
    hi = x.astype(_BF16)
    lo = (x - hi.astype(_F32)).astype(_BF16)
    return jnp.dot(jnp.concatenate([hi, lo], axis=1), w2, preferred_element_type=_F32)


def _rms_heads(x, g):
    return x * lax.rsqrt(jnp.mean(x * x, axis=-1, keepdims=True) + EPS) * g


def _attn_block(q, k, v, bias):
    s = lax.dot_general(q.astype(_BF16), k.astype(_BF16), (((1,), (1,)), ((), ())),
                        preferred_element_type=_F32) + bias
    m = jnp.max(s, axis=-1, keepdims=True)
    p = jnp.exp(s - m).astype(_BF16)
    v1 = jnp.concatenate([v.astype(_BF16), jnp.ones(v.shape, _BF16)], axis=1)
    acc = jnp.dot(p, v1, preferred_element_type=_F32)
    l = acc[:, LANES:]
    return acc[:, :LANES] * (1.0 / l), m + jnp.log(l)


def _attn_prompt_kernel(*refs, seq, n_alias):
    q0_ref, q1_ref, q2_ref, k_ref, v_ref, zb_ref, qg_ref, kg_ref, inv_ref = refs[:9]
    yb_ref, ko_ref, vo_ref = refs[9 + n_alias:12 + n_alias]
    cos_ref, sin_ref, band_ref, causal_ref, qs_ref, og_ref, lg_ref = refs[12 + n_alias:]
    rows = 256
    n_chunks = seq // rows

    @pl.when((pl.program_id(0) == 0) & (pl.program_id(1) == 0))
    def _():
        def body(c, carry):
            r0 = pl.multiple_of(c * rows, rows)
            pos = (lax.broadcasted_iota(jnp.int32, (rows, LANES), 0) + r0).astype(_F32)
            lane = lax.broadcasted_iota(jnp.int32, (rows, LANES), 1)
            cos_t, sin_t = _rope_tables(pos, lane, inv_ref[...])
            cos_ref[pl.ds(r0, rows), :] = cos_t
            sin_ref[pl.ds(r0, rows), :] = sin_t
            return carry
        lax.fori_loop(0, n_chunks, body, 0)
        ri = lax.broadcasted_iota(jnp.int32, (QB, 2 * QB), 0)
        ci = lax.broadcasted_iota(jnp.int32, (QB, 2 * QB), 1)
        band_ref[...] = jnp.where(ci < ri, NEG, jnp.where(ci > ri + QB, NEG, 0.0))
        rq = lax.broadcasted_iota(jnp.int32, (QB, QB), 0)
        cq = lax.broadcasted_iota(jnp.int32, (QB, QB), 1)
        causal_ref[...] = jnp.where(cq > rq, NEG, 0.0)

    scale = HEAD_DIM ** -0.5
    rot_m = _rotate_half_matrix().astype(_BF16)
    rot2 = jnp.concatenate([rot_m, rot_m], axis=0)
    ones2 = jnp.ones((2 * LANES, LANES), _BF16)

    def norm_rope(x, g, cos_t, sin_t):
        ssq = _dot_hi_lo(x * x, ones2)
        y = x * lax.rsqrt(ssq * (1.0 / HEAD_DIM) + EPS) * g
        return y * cos_t + _dot_hi_lo(y, rot2) * sin_t

    def prep(c, carry):
        r0 = pl.multiple_of(c * rows, rows)
        sl = pl.ds(r0, rows)
        cos_t, sin_t = cos_ref[sl, :], sin_ref[sl, :]
        for g, q_ref in enumerate((q0_ref, q1_ref, q2_ref)):
            qs_ref[g, sl, :] = norm_rope(q_ref[0, sl, :], qg_ref[...], cos_t, sin_t) * scale
        ko_ref[0, 0, sl, :] = norm_rope(k_ref[0, sl, :], kg_ref[...], cos_t, sin_t)
        vo_ref[0, 0, sl, :] = v_ref[0, sl, :]
        for l2 in range(1, ko_ref.shape[0]):
            ko_ref[l2, 0, sl, :] = jnp.zeros((rows, LANES), _F32)
            vo_ref[l2, 0, sl, :] = jnp.zeros((rows, LANES), _F32)
        return carry
    lax.fori_loop(0, n_chunks, prep, 0)

    def rows_of(start, n, d):
        return pl.ds(start, n) if d == 1 else pl.ds(start, n, stride=d)

    for g, (w, d) in enumerate(GROUPS):
        assert w // d == QB
        n_blk = seq // (d * QB)
        for r in range(d):
            for blk in range(n_blk):
                q = qs_ref[g, rows_of(r + d * QB * blk, QB, d), :]
                if blk == 0:
                    ksl, bias = rows_of(r, QB, d), causal_ref[...]
                else:
                    ksl, bias = rows_of(r + d * QB * (blk - 1), 2 * QB, d), band_ref[...]
                o, lse = _attn_block(q, ko_ref[0, 0, ksl, :], v_ref[0, ksl, :], bias)
                osl = rows_of(r + d * QB * blk, QB, d)
                og_ref[g, osl, :] = o
                lg_ref[g, osl, :] = lse

    def merge(c, carry):
        r0 = pl.multiple_of(c * rows, rows)
        sl = pl.ds(r0, rows)
        l0, l1, l2 = lg_ref[0, sl, :], lg_ref[1, sl, :], lg_ref[2, sl, :]
        mx = jnp.maximum(jnp.maximum(l0, l1), l2)
        w0, w1, w2 = jnp.exp(l0 - mx), jnp.exp(l1 - mx), jnp.exp(l2 - mx)
        o = (w0 * og_ref[0, sl, :] + w1 * og_ref[1, sl, :] + w2 * og_ref[2, sl, :]) / (w0 + w1 + w2)
        yb_ref[0, sl, :] = (o * _silu(zb_ref[0, sl, :])).astype(_BF16)
        return carry
    lax.fori_loop(0, n_chunks, merge, 0)


def _attn_prompt(h3, qg, kg, inv, layer, kv_bufs):
    b, seq, _ = h3.shape
    hb = BR_W // LANES
    first = kv_bufs is None
    assert first == (layer == 0)

    def col(cb):
        return pl.BlockSpec((1, seq, LANES), lambda i, j: (i, 0, cb * hb + j))

    small = pl.BlockSpec((1, LANES), lambda i, j: (0, 0))
    any_spec = pl.BlockSpec(memory_space=pl.ANY)
    kv_out = pl.BlockSpec((DEPTH if first else 1, 1, seq, LANES), lambda i, j: (layer, i, 0, j))
    kv_shape = jax.ShapeDtypeStruct((DEPTH, b, seq, BR_W), _F32)
    n_alias = 0 if first else 2
    return pl.pallas_call(
        functools.partial(_attn_prompt_kernel, seq=seq, n_alias=n_alias),
        grid=(b, N_HEADS),
        in_specs=[col(COL_Q), col(COL_Q + 1), col(COL_Q + 2), col(COL_K), col(COL_V), col(COL_ZB),
                  small, small, small] + [any_spec] * n_alias,
        out_specs=[pl.BlockSpec((1, seq, LANES), lambda i, j: (i, 0, j)), kv_out, kv_out],
        out_shape=[jax.ShapeDtypeStruct((b, seq, BR_W), _BF16), kv_shape, kv_shape],
        input_output_aliases={} if first else {9: 1, 10: 2},
        scratch_shapes=[pltpu.VMEM((seq, LANES), _F32)] * 2
        + [pltpu.VMEM((QB, 2 * QB), _F32), pltpu.VMEM((QB, QB), _F32)]
        + [pltpu.VMEM((N_GROUPS, seq, LANES), _F32)] * 3,
        compiler_params=_params(("arbitrary", "arbitrary")),
        name="attn_prompt",
    )(h3, h3, h3, h3, h3, h3, qg, kg, inv, *(() if first else kv_bufs))


def _attn_sample_kernel(*refs, t_len, n_rows, layer, n_alias):
    (q0_ref, q1_ref, q2_ref, k_ref, v_ref, zb_ref, qg_ref, kg_ref, inv_ref, kc_ref, vc_ref,
     ck_hbm, cv_hbm) = refs[:13]
    yb_ref, ok_hbm, ov_hbm = refs[13 + n_alias:16 + n_alias]
    kn_ref, vn_ref, zero_ref, sem = refs[16 + n_alias:]
    first = n_alias == 0
    b = pl.program_id(0)
    keep = n_rows - t_len
    copies = []

    def dma(src, dst):
        cp = pltpu.make_async_copy(src, dst, sem.at[len(copies)])
        cp.start()
        copies.append(cp)

    if first:
        for l2 in range(DEPTH):
            dma(ck_hbm.at[l2, b, pl.ds(t_len, keep)], ok_hbm.at[l2, b, pl.ds(0, keep)])
            dma(cv_hbm.at[l2, b, pl.ds(t_len, keep)], ov_hbm.at[l2, b, pl.ds(0, keep)])

    shape = (t_len, N_HEADS, LANES)
    pos = (lax.broadcasted_iota(jnp.int32, shape, 0) + PAST_LEN).astype(_F32)
    lane = lax.broadcasted_iota(jnp.int32, shape, 2)
    cos_t, sin_t = _rope_tables(pos, lane, inv_ref[...].reshape(1, 1, LANES))

    def norm_rope(x, g_ref):
        y = _rms_heads(x, g_ref[...].reshape(1, 1, LANES))
        return y * cos_t + _rotate_half_lanes(y) * sin_t

    scale = HEAD_DIM ** -0.5
    k_new = norm_rope(k_ref[0], kg_ref)
    v_new = v_ref[0]
    kn_ref[...] = k_new
    vn_ref[...] = v_new
    zero_ref[...] = jnp.zeros(shape, _F32)
    tail = pl.ds(keep, t_len)
    dma(kn_ref, ok_hbm.at[layer, b, tail])
    dma(vn_ref, ov_hbm.at[layer, b, tail])
    if first:
        for l2 in range(DEPTH):
            if l2 != layer:
                dma(zero_ref, ok_hbm.at[l2, b, tail])
                dma(zero_ref, ov_hbm.at[l2, b, tail])

    nj = n_rows // 16
    qs = [norm_rope(q_ref[0], qg_ref) * scale for q_ref in (q0_ref, q1_ref, q2_ref)]

    for t in range(t_len):
        outs, lses = [], []
        for g, (w, d) in enumerate(GROUPS):
            q = qs[g][t]
            pieces = []
            if d == 1:
                lo = nj - QB // 16
                kc = kc_ref[0, 0, lo:nj].reshape(QB, N_HEADS, LANES)
                vc = vc_ref[0, 0, lo:nj].reshape(QB, N_HEADS, LANES)
                idx = lax.broadcasted_iota(jnp.int32, (QB, N_HEADS, 1), 0)
                pieces.append((kc, vc, idx >= t))
                pieces.append((k_new[:t + 1], v_new[:t + 1], None))
            else:
                per = 16 // d
                lo = nj - (w // d) // per
                for m in range(per):
                    pieces.append((kc_ref[0, 0, lo:nj, t + d * m], vc_ref[0, 0, lo:nj, t + d * m], None))
                pieces.append((k_new[t:t + 1], v_new[t:t + 1], None))
            scores = []
            for kk, vv, mask in pieces:
                s = jnp.sum(kk * q[None], axis=-1, keepdims=True)
                if mask is not None:
                    s = jnp.where(mask, s, NEG)
                scores.append(s)
            mx = functools.reduce(jnp.maximum, [jnp.max(s, axis=0) for s in scores])
            den = 0.0
            acc = 0.0
            for s, (kk, vv, mask) in zip(scores, pieces):
                p = jnp.exp(s - mx[None])
                den = den + jnp.sum(p, axis=0)
                acc = acc + jnp.sum(p * vv, axis=0)
            outs.append(acc / den)
            lses.append(mx + jnp.log(den))
        mx = jnp.maximum(jnp.maximum(lses[0], lses[1]), lses[2])
        ws = [jnp.exp(l - mx) for l in lses]
        o = (ws[0] * outs[0] + ws[1] * outs[1] + ws[2] * outs[2]) / (ws[0] + ws[1] + ws[2])
        yb_ref[0, t] = o * _silu(zb_ref[0, t])

    for cp in copies:
        cp.wait()


def _attn_sample(h4, qg, kg, inv, cache_k, cache_v, layer, kv_bufs):
    b, t_len = h4.shape[:2]
    n_rows = cache_k.shape[2]
    first = kv_bufs is None
    assert first == (layer == 0)
    kc = cache_k.reshape(DEPTH, b, n_rows // 16, 16, N_HEADS, LANES)
    vc = cache_v.reshape(DEPTH, b, n_rows // 16, 16, N_HEADS, LANES)

    def col(cb):
        return pl.BlockSpec((1, t_len, N_HEADS, LANES), lambda i: (i, 0, cb, 0))

    small = pl.BlockSpec((1, LANES), lambda i: (0, 0))
    cache = pl.BlockSpec((1, 1, n_rows // 16, 16, N_HEADS, LANES), lambda i: (layer, i, 0, 0, 0, 0))
    any_spec = pl.BlockSpec(memory_space=pl.ANY)
    n_alias = 0 if first else 2
    rows_shape = (t_len, N_HEADS, LANES)
    n_dma = 2 * DEPTH + 2 + 2 * (DEPTH - 1)
    return pl.pallas_call(
        functools.partial(_attn_sample_kernel, t_len=t_len, n_rows=n_rows, layer=layer, n_alias=n_alias),
        grid=(b,),
        in_specs=[col(COL_Q), col(COL_Q + 1), col(COL_Q + 2), col(COL_K), col(COL_V), col(COL_ZB),
                  small, small, small, cache, cache, any_spec, any_spec] + [any_spec] * n_alias,
        out_specs=[pl.BlockSpec((1, t_len, N_HEADS, LANES), lambda i: (i, 0, 0, 0)), any_spec, any_spec],
        out_shape=[jax.ShapeDtypeStruct((b, t_len, N_HEADS, LANES), _F32),
                   jax.ShapeDtypeStruct(cache_k.shape, _F32), jax.ShapeDtypeStruct(cache_v.shape, _F32)],
        input_output_aliases={} if first else {13: 1, 14: 2},
        scratch_shapes=[pltpu.VMEM(rows_shape, _F32)] * 3 + [pltpu.SemaphoreType.DMA((n_dma,))],
        compiler_params=_params(("arbitrary",)),
        name="attn_sample",
    )(h4, h4, h4, h4, h4, h4, qg, kg, inv, kc, vc, cache_k, cache_v, *(() if first else kv_bufs))


def _branch_kernel(va_ref, ca_ref, ba_ref, za_ref, uc_ref, zc_ref, ga_ref, gb_ref, zd_ref,
                   sta_ref, stc_ref, std_ref, aw_ref, pw_ref, cs_ref, dw_ref, db_ref, lg_ref, lb_ref,
                   ya_ref, yc_ref, yd_ref, na_ref, nc_ref, nd_ref,
                   ea_ref, ec_ref, eds_ref, dacc_ref, pp_ref, *, tm, pos0):
    i = pl.program_id(1)
    ed_ref = eds_ref.at[0]

    @pl.when(i == 0)
    def _():
        ea_ref[0:HALO, :] = sta_ref[0]
        ec_ref[0:HALO, :] = stc_ref[0]
        ed_ref[0:HALO, :] = std_ref[0]
        pp_ref[...] = jnp.zeros_like(pp_ref)

    @pl.when(i > 0)
    def _():
        for e_ref in (ea_ref, ec_ref, ed_ref):
            e_ref[0:HALO, :] = e_ref[tm:tm + HALO, :]

    new = pl.ds(HALO, tm)
    ea_ref[new, :] = ca_ref[0] * va_ref[0]
    ec_ref[new, :] = uc_ref[0]
    ed_ref[new, :] = ga_ref[0] * _sigmoid(gb_ref[0])
    n_shift = HALO + tm - 8
    for r in range(1, 8):
        eds_ref[r, 0:n_shift, :] = ed_ref[pl.ds(r, n_shift), :]

    conv = aw_ref[0:1, :] * ea_ref[pl.ds(HALO - 2, tm), :]
    for k in range(1, A_CONV):
        conv = conv + aw_ref[k:k + 1, :] * ea_ref[pl.ds(HALO - 2 + k, tm), :]
    ya_ref[0] = (ba_ref[0] * conv * _silu(za_ref[0])).astype(_BF16)

    pos = pos0 + i * tm + lax.broadcasted_iota(jnp.int32, (tm, C_GROUP), 0)
    for g, w in enumerate(POOL_WINDOWS):
        cols = slice(g * C_GROUP, (g + 1) * C_GROUP)
        tok = ec_ref[new, cols]
        s = tok
        for j in range(1, w):
            s = s + ec_ref[pl.ds(HALO - j, tm), cols]
        cnt = jnp.minimum(pos + 1, w).astype(_F32)
        pp_ref[0:tm, :] = s / cnt - tok
        y = jnp.dot(pp_ref[...].astype(_BF16), pw_ref[g], preferred_element_type=_F32)[0:tm]
        yc_ref[0, :, cols] = ((y * cs_ref[:, cols]) * _silu(zc_ref[0, :, cols])).astype(_BF16)

    rc = min(tm, 128)
    for c in range(BR_W // LANES):
        cols = slice(c * LANES, (c + 1) * LANES)
        for r0 in range(0, tm, rc):
            acc = None
            for k in range(D_CONV):
                off = HALO - (D_CONV - 1) + k
                term = dw_ref[k:k + 1, cols] * eds_ref[off % 8, pl.ds(off // 8 * 8 + r0, rc), cols]
                acc = term if acc is None else acc + term
            dacc_ref[pl.ds(r0, rc), cols] = acc + db_ref[:, cols]
    x = dacc_ref[0:tm, :]
    mu = jnp.mean(x, axis=-1, keepdims=True)
    xc = x - mu
    var = jnp.mean(xc * xc, axis=-1, keepdims=True)
    y = xc * lax.rsqrt(var + EPS) * lg_ref[...] + lb_ref[...]
    yd_ref[0] = (_silu(y) * _silu(zd_ref[0])).astype(_BF16)

    @pl.when(i == pl.num_programs(1) - 1)
    def _():
        na_ref[0] = ea_ref[pl.ds(HALO + tm - (A_CONV - 1), A_CONV - 1), :]
        nc_ref[0] = ec_ref[pl.ds(HALO + tm - POOL_PAST, POOL_PAST), :]
        nd_ref[0] = ed_ref[pl.ds(HALO + tm - (D_CONV - 1), D_CONV - 1), :]


def _branches(h3, states, wl, pos0, tm):
    b, t_len, _ = h3.shape

    def col(cb):
        return pl.BlockSpec((1, tm, BR_W), lambda i, j: (i, j, cb))

    def full(shape):
        return pl.BlockSpec(shape, lambda i, j: (0,) * len(shape))

    state = pl.BlockSpec((1, HALO, BR_W), lambda i, j: (i, 0, 0))
    y_out = pl.BlockSpec((1, tm, BR_W), lambda i, j: (i, j, 0))

    def st_out(n):
        return pl.BlockSpec((1, n, BR_W), lambda i, j: (i, 0, 0))

    tmc = max(tm, 8)
    n_pool = len(POOL_WINDOWS)
    return pl.pallas_call(
        functools.partial(_branch_kernel, tm=tm, pos0=pos0),
        grid=(b, t_len // tm),
        in_specs=[col(COL_VA), col(COL_CA), col(COL_BA), col(COL_ZA), col(COL_UC), col(COL_ZC),
                  col(COL_GA), col(COL_GB), col(COL_ZD), state, state, state,
                  full((A_CONV, BR_W)), full((n_pool, C_GROUP, C_GROUP)), full((1, BR_W)),
                  full((D_CONV, BR_W)), full((1, BR_W)), full((1, BR_W)), full((1, BR_W))],
        out_specs=[y_out, y_out, y_out, st_out(A_CONV - 1), st_out(POOL_PAST), st_out(D_CONV - 1)],
        out_shape=[jax.ShapeDtypeStruct((b, t_len, BR_W), _BF16)] * 3
        + [jax.ShapeDtypeStruct((b, n, BR_W), _F32) for n in (A_CONV - 1, POOL_PAST, D_CONV - 1)],
        scratch_shapes=[pltpu.VMEM((HALO + tmc, BR_W), _F32)] * 2 + [pltpu.VMEM((8, HALO + tmc, BR_W), _F32)]
        + [pltpu.VMEM((tmc, BR_W), _F32), pltpu.VMEM((tmc, C_GROUP), _F32)],
        compiler_params=_params(("arbitrary", "arbitrary")),
        name="branches",
    )(*([h3] * 9), *states, wl["a_conv_w"], wl["c_pool_w"], wl["c_scale"], wl["d_conv_w"],
      wl["d_conv_b"], wl["d_ln_g"], wl["d_ln_b"])


def _merge_kernel(ya_ref, yb_ref, yc_ref, yd_ref, wa_ref, wb_ref, wc_ref, wd_ref,
                  g0_ref, g1_ref, g2_ref, g3_ref, o_ref):
    acc = None
    for y_ref, w_ref, g_ref in ((ya_ref, wa_ref, g0_ref), (yb_ref, wb_ref, g1_ref),
                                (yc_ref, wc_ref, g2_ref), (yd_ref, wd_ref, g3_ref)):
        term = _sigmoid(g_ref[...]) * jnp.dot(y_ref[...].astype(_BF16), w_ref[...],
                                                    preferred_element_type=_F32)
        acc = term if acc is None else acc + term
    o_ref[...] = acc.astype(_BF16)


def _merge(ys, w_brs, h2, layer, tm, tn=512):
    m = h2.shape[0]
    gate0 = COL_GATE * BR_W // tn
    y_spec = pl.BlockSpec((tm, BR_W), lambda i, j: (i, 0))
    w_spec = pl.BlockSpec((None, BR_W, tn), lambda i, j: (layer, 0, j))

    def gate(n):
        return pl.BlockSpec((tm, tn), lambda i, j: (i, gate0 + n * (D_MODEL // tn) + j))

    return pl.pallas_call(
        _merge_kernel,
        grid=(m // tm, D_MODEL // tn),
        in_specs=[y_spec] * 4 + [w_spec] * 4 + [gate(n) for n in range(N_BRANCH)],
        out_specs=pl.BlockSpec((tm, tn), lambda i, j: (i, j)),
        out_shape=jax.ShapeDtypeStruct((m, D_MODEL), _BF16),
        compiler_params=_params(("parallel", "arbitrary")),
        name="merge",
    )(*ys, *w_brs, h2, h2, h2, h2)


def _outproj_kernel(m_ref, w_ref, x_ref, o_ref):
    o_ref[...] = x_ref[...] + jnp.dot(m_ref[...], w_ref[...], preferred_element_type=_F32)


def _outproj(merged, w_out, x2, layer, tm, tn=1024):
    m = x2.shape[0]
    return pl.pallas_call(
        _outproj_kernel,
        grid=(m // tm, D_MODEL // tn),
        in_specs=[pl.BlockSpec((tm, D_MODEL), lambda i, j: (i, 0)),
                  pl.BlockSpec((None, D_MODEL, tn), lambda i, j: (layer, 0, j)),
                  pl.BlockSpec((tm, tn), lambda i, j: (i, j))],
        out_specs=pl.BlockSpec((tm, tn), lambda i, j: (i, j)),
        out_shape=jax.ShapeDtypeStruct((m, D_MODEL), _F32),
        compiler_params=_params(("parallel", "arbitrary")),
        name="outproj",
    )(merged, w_out, x2)


def _layer(x3, layer, past, wl, big, kv_bufs):
    b, t_len, _ = x3.shape
    m = b * t_len
    x2 = x3.reshape(m, D_MODEL)
    tm_mat = min(m, 1024)
    h2 = _inproj(x2, wl["norm_g"], big["w_in"], layer, tm_mat)
    h3 = h2.reshape(b, t_len, N_IN)

    if past is None:
        yb, k_buf, v_buf = _attn_prompt(h3, wl["q_norm_g"], wl["k_norm_g"], wl["inv"], layer, kv_bufs)
        states = [jnp.zeros((b, HALO, BR_W), _F32)] * 3
        pos0, tm_br = 0, 256
    else:
        h4 = h2.reshape(b, t_len, N_IN // LANES, LANES)
        yb, k_buf, v_buf = _attn_sample(h4, wl["q_norm_g"], wl["k_norm_g"], wl["inv"], past[0], past[1],
                                        layer, kv_bufs)
        states = [jnp.pad(s[layer], ((0, 0), (HALO - s.shape[2], 0), (0, 0))) for s in past[2:]]
        pos0, tm_br = PAST_LEN, t_len
    yb = yb.reshape(m, BR_W)

    ya, yc, yd, new_a, new_c, new_d = _branches(h3, states, wl, pos0, tm_br)
    ys = (ya.reshape(m, BR_W), yb, yc.reshape(m, BR_W), yd.reshape(m, BR_W))
    merged = _merge(ys, big["w_br"], h2, layer, tm_mat)
    out = _outproj(merged, big["w_out"], x2, layer, tm_mat)
    return out.reshape(b, t_len, D_MODEL), (k_buf, v_buf), (new_a, new_c, new_d)


def kernel(x_prompt, x_sample, cache_attn_k, cache_attn_v, state_conv_a, state_pool_c, state_conv_d,
           norm_g, w_in, q_norm_g, k_norm_g, a_conv_w, c_pool_w, c_scale, d_conv_w, d_conv_b,
           d_ln_g, d_ln_b, w_br_a, w_br_b, w_br_c, w_br_d, w_out):
    half = ROT_DIM // 2
    inv = ROPE_THETA ** (-(jnp.arange(half, dtype=_F32) / half))
    inv = jnp.concatenate([inv, inv, jnp.zeros((LANES - ROT_DIM,), _F32)])[None, :]
    big = {"w_in": w_in.astype(_BF16), "w_out": w_out.astype(_BF16),
           "w_br": tuple(w.astype(_BF16) for w in (w_br_a, w_br_b, w_br_c, w_br_d))}
    pool_w = c_pool_w.astype(_BF16)
    past = (cache_attn_k, cache_attn_v, state_conv_a, state_pool_c, state_conv_d)

    hp, hs = x_prompt, x_sample
    kv_p = kv_s = None
    st_p = [[] for _ in range(3)]
    st_s = [[] for _ in range(3)]
    for l in range(DEPTH):
        wl = {"norm_g": norm_g[l][None], "q_norm_g": q_norm_g[l][None], "k_norm_g": k_norm_g[l][None],
              "inv": inv, "a_conv_w": a_conv_w[l], "c_pool_w": pool_w[l], "c_scale": c_scale[l][None],
              "d_conv_w": d_conv_w[l], "d_conv_b": d_conv_b[l][None], "d_ln_g": d_ln_g[l][None],
              "d_ln_b": d_ln_b[l][None]}
        hp, kv_p, sp = _layer(hp, l, None, wl, big, kv_p)
        hs, kv_s, ss = _layer(hs, l, past, wl, big, kv_s)
        for n in range(3):
            st_p[n].append(sp[n])
            st_s[n].append(ss[n])
    b, seq = x_prompt.shape[:2]
    kv_p = [a.reshape(DEPTH, b, seq, N_HEADS, HEAD_DIM) for a in kv_p]
    return (hp, hs, *kv_p, *(jnp.stack(s, axis=0) for s in st_p),
            *kv_s, *(jnp.stack(s, axis=0) for s in st_s))
```

```python
import functools

import jax
import jax.numpy as jnp
from jax import lax
from jax.experimental import pallas as pl
from jax.experimental.pallas import tpu as pltpu

D_MODEL = 2048
DEPTH = 2
PAST_LEN = 16384
BR_W = D_MODEL // 2
N_BRANCH = 4
A_CONV = 3
HEAD_DIM = 128
N_HEADS = BR_W // HEAD_DIM
GROUPS = ((128, 1), (512, 4), (2048, 16))
N_GROUPS = len(GROUPS)
ROT_DIM = HEAD_DIM // 4
ROPE_THETA = 500000.0
POOL_WINDOWS = (2, 4, 8, 16)
C_GROUP = BR_W // len(POOL_WINDOWS)
POOL_PAST = POOL_WINDOWS[-1] - 1
D_CONV = 31
EPS = 1e-6
N_IN = 4 * BR_W + (N_GROUPS + 3) * BR_W + 2 * BR_W + 3 * BR_W + N_BRANCH * D_MODEL

COL_VA, COL_CA, COL_BA, COL_ZA = 0, 1, 2, 3
COL_Q, COL_K, COL_V, COL_ZB = 4, 7, 8, 9
COL_UC, COL_ZC = 10, 11
COL_GA, COL_GB, COL_ZD = 12, 13, 14
COL_GATE = 15

LANES = 128
QB = 128
HALO = 32
NEG = -1e30
VMEM_LIMIT = 56 * 1024 * 1024

_BF16 = jnp.bfloat16
_F32 = jnp.float32


def _sigmoid(x):
    return 0.5 * jnp.tanh(0.5 * x) + 0.5


def _silu(x):
    return x * _sigmoid(x)


def _params(sem):
    return pltpu.CompilerParams(dimension_semantics=sem, vmem_limit_bytes=VMEM_LIMIT)


def _inproj_kernel(x_ref, g_ref, w_ref, h_ref, xn_ref):
    @pl.when(pl.program_id(1) == 0)
    def _():
        x = x_ref[...]
        y = x * lax.rsqrt(jnp.mean(x * x, axis=-1, keepdims=True) + EPS)
        xn_ref[...] = (y * g_ref[...]).astype(_BF16)

    h_ref[...] = jnp.dot(xn_ref[...], w_ref[...], preferred_element_type=_F32)


def _inproj(x2, g, w_bf16, tm, tn=1024):
    m = x2.shape[0]
    return pl.pallas_call(
        _inproj_kernel,
        grid=(m // tm, N_IN // tn),
        in_specs=[
            pl.BlockSpec((tm, D_MODEL), lambda i, j: (i, 0)),
            pl.BlockSpec((1, D_MODEL), lambda i, j: (0, 0)),
            pl.BlockSpec((D_MODEL, tn), lambda i, j: (0, j)),
        ],
        out_specs=pl.BlockSpec((tm, tn), lambda i, j: (i, j)),
        out_shape=jax.ShapeDtypeStruct((m, N_IN), _F32),
        scratch_shapes=[pltpu.VMEM((tm, D_MODEL), _BF16)],
        compiler_params=_params(("parallel", "arbitrary")),
        name="inproj",
    )(x2, g, w_bf16)


def _inproj_cast_kernel(x_ref, g_ref, w_ref, h_ref, wb_ref, xn_ref):
    @pl.when(pl.program_id(0) == 0)
    def _():
        x = x_ref[...]
        y = x * lax.rsqrt(jnp.mean(x * x, axis=-1, keepdims=True) + EPS)
        xn_ref[...] = (y * g_ref[...]).astype(_BF16)

    wb = w_ref[...].astype(_BF16)
    wb_ref[...] = wb
    h_ref[...] = jnp.dot(xn_ref[...], wb, preferred_element_type=_F32)


def _inproj_cast(x2, g, w_in, layer, tn=1024):
    m = x2.shape[0]
    return pl.pallas_call(
        _inproj_cast_kernel,
        grid=(N_IN // tn,),
        in_specs=[
            pl.BlockSpec((m, D_MODEL), lambda j: (0, 0)),
            pl.BlockSpec((1, D_MODEL), lambda j: (0, 0)),
            pl.BlockSpec((None, D_MODEL, tn), lambda j: (layer, 0, j)),
        ],
        out_specs=[pl.BlockSpec((m, tn), lambda j: (0, j)), pl.BlockSpec((D_MODEL, tn), lambda j: (0, j))],
        out_shape=[jax.ShapeDtypeStruct((m, N_IN), _F32), jax.ShapeDtypeStruct((D_MODEL, N_IN), _BF16)],
        scratch_shapes=[pltpu.VMEM((m, D_MODEL), _BF16)],
        compiler_params=_params(("arbitrary",)),
        name="inproj_cast",
    )(x2, g, w_in)


def _rope_tables(pos, lane, inv):
    ang = pos * inv
    cos_t = jnp.where(lane < ROT_DIM, jnp.cos(ang), 1.0)
    sin_t = jnp.where(lane < ROT_DIM, jnp.sin(ang), 0.0)
    return cos_t, sin_t


def _rotate_half_matrix():
    half = ROT_DIM // 2
    m = lax.broadcasted_iota(jnp.int32, (LANES, LANES), 0)
    l = lax.broadcasted_iota(jnp.int32, (LANES, LANES), 1)
    return jnp.where(m == l + half, jnp.where(l < half, -1.0, 0.0),
                     jnp.where(m == l - half, jnp.where(l < ROT_DIM, 1.0, 0.0), 0.0))


def _rotate_half_lanes(x):
    ax = x.ndim - 1
    half = ROT_DIM // 2
    lane = lax.broadcasted_iota(jnp.int32, x.shape, ax)
    return jnp.where(lane < half, -pltpu.roll(x, LANES - half, ax), pltpu.roll(x, half, ax))


def _dot_hi_lo(x, w2):
    hi = x.astype(_BF16)
    lo = (x - hi.astype(_F32)).astype(_BF16)
    return jnp.dot(jnp.concatenate([hi, lo], axis=1), w2, preferred_element_type=_F32)


def _rms_heads(x, g):
    return x * lax.rsqrt(jnp.mean(x * x, axis=-1, keepdims=True) + EPS) * g


def _attn_block(q, k, v, bias):
    s = lax.dot_general(q.astype(_BF16), k.astype(_BF16), (((1,), (1,)), ((), ())),
                        preferred_element_type=_F32) + bias
    m = jnp.max(s, axis=-1, keepdims=True)
    p = jnp.exp(s - m).astype(_BF16)
    v1 = jnp.concatenate([v.astype(_BF16), jnp.ones(v.shape, _BF16)], axis=1)
    acc = jnp.dot(p, v1, preferred_element_type=_F32)
    l = acc[:, LANES:]
    return acc[:, :LANES] * (1.0 / l), m + jnp.log(l)


def _attn_prompt_kernel(*refs, seq, n_alias):
    q0_ref, q1_ref, q2_ref, k_ref, v_ref, zb_ref, qg_ref, kg_ref, inv_ref = refs[:9]
    yb_ref, ko_ref, vo_ref = refs[9 + n_alias:12 + n_alias]
    cos_ref, sin_ref, band_ref, causal_ref, qs_ref, og_ref, lg_ref = refs[12 + n_alias:]
    rows = 256
    n_chunks = seq // rows

    @pl.when((pl.program_id(0) == 0) & (pl.program_id(1) == 0))
    def _():
        def body(c, carry):
            r0 = pl.multiple_of(c * rows, rows)
            pos = (lax.broadcasted_iota(jnp.int32, (rows, LANES), 0) + r0).astype(_F32)
            lane = lax.broadcasted_iota(jnp.int32, (rows, LANES), 1)
            cos_t, sin_t = _rope_tables(pos, lane, inv_ref[...])
            cos_ref[pl.ds(r0, rows), :] = cos_t
            sin_ref[pl.ds(r0, rows), :] = sin_t
            return carry
        lax.fori_loop(0, n_chunks, body, 0)
        ri = lax.broadcasted_iota(jnp.int32, (QB, 2 * QB), 0)
        ci = lax.broadcasted_iota(jnp.int32, (QB, 2 * QB), 1)
        band_ref[...] = jnp.where(ci < ri, NEG, jnp.where(ci > ri + QB, NEG, 0.0))
        rq = lax.broadcasted_iota(jnp.int32, (QB, QB), 0)
        cq = lax.broadcasted_iota(jnp.int32, (QB, QB), 1)
        causal_ref[...] = jnp.where(cq > rq, NEG, 0.0)

    scale = HEAD_DIM ** -0.5
    rot_m = _rotate_half_matrix().astype(_BF16)
    rot2 = jnp.concatenate([rot_m, rot_m], axis=0)
    ones2 = jnp.ones((2 * LANES, LANES), _BF16)

    def norm_rope(x, g, cos_t, sin_t):
        ssq = _dot_hi_lo(x * x, ones2)
        y = x * lax.rsqrt(ssq * (1.0 / HEAD_DIM) + EPS) * g
        return y * cos_t + _dot_hi_lo(y, rot2) * sin_t

    def prep(c, carry):
        r0 = pl.multiple_of(c * rows, rows)
        sl = pl.ds(r0, rows)
        cos_t, sin_t = cos_ref[sl, :], sin_ref[sl, :]
        for g, q_ref in enumerate((q0_ref, q1_ref, q2_ref)):
            qs_ref[g, sl, :] = norm_rope(q_ref[0, sl, :], qg_ref[...], cos_t, sin_t) * scale
        ko_ref[0, 0, sl, :] = norm_rope(k_ref[0, sl, :], kg_ref[...], cos_t, sin_t)
        vo_ref[0, 0, sl, :] = v_ref[0, sl, :]
        for l2 in range(1, ko_ref.shape[0]):
            ko_ref[l2, 0, sl, :] = jnp.zeros((rows, LANES), _F32)
            vo_ref[l2, 0, sl, :] = jnp.zeros((rows, LANES), _F32)
        return carry
    lax.fori_loop(0, n_chunks, prep, 0)

    def rows_of(start, n, d):
        return pl.ds(start, n) if d == 1 else pl.ds(start, n, stride=d)

    for g, (w, d) in enumerate(GROUPS):
        assert w // d == QB
        n_blk = seq // (d * QB)
        for r in range(d):
            for blk in range(n_blk):
                q = qs_ref[g, rows_of(r + d * QB * blk, QB, d), :]
                if blk == 0:
                    ksl, bias = rows_of(r, QB, d), causal_ref[...]
                else:
                    ksl, bias = rows_of(r + d * QB * (blk - 1), 2 * QB, d), band_ref[...]
                o, lse = _attn_block(q, ko_ref[0, 0, ksl, :], v_ref[0, ksl, :], bias)
                osl = rows_of(r + d * QB * blk, QB, d)
                og_ref[g, osl, :] = o
                lg_ref[g, osl, :] = lse

    def merge(c, carry):
        r0 = pl.multiple_of(c * rows, rows)
        sl = pl.ds(r0, rows)
        l0, l1, l2 = lg_ref[0, sl, :], lg_ref[1, sl, :], lg_ref[2, sl, :]
        mx = jnp.maximum(jnp.maximum(l0, l1), l2)
        w0, w1, w2 = jnp.exp(l0 - mx), jnp.exp(l1 - mx), jnp.exp(l2 - mx)
        o = (w0 * og_ref[0, sl, :] + w1 * og_ref[1, sl, :] + w2 * og_ref[2, sl, :]) / (w0 + w1 + w2)
        yb_ref[0, sl, :] = (o * _silu(zb_ref[0, sl, :])).astype(_BF16)
        return carry
    lax.fori_loop(0, n_chunks, merge, 0)


def _attn_prompt(h3, qg, kg, inv, layer, kv_bufs):
    b, seq, _ = h3.shape
    hb = BR_W // LANES
    first = kv_bufs is None
    assert first == (layer == 0)

    def col(cb):
        return pl.BlockSpec((1, seq, LANES), lambda i, j: (i, 0, cb * hb + j))

    small = pl.BlockSpec((1, LANES), lambda i, j: (0, 0))
    any_spec = pl.BlockSpec(memory_space=pl.ANY)
    kv_out = pl.BlockSpec((DEPTH if first else 1, 1, seq, LANES), lambda i, j: (layer, i, 0, j))
    kv_shape = jax.ShapeDtypeStruct((DEPTH, b, seq, BR_W), _F32)
    n_alias = 0 if first else 2
    return pl.pallas_call(
        functools.partial(_attn_prompt_kernel, seq=seq, n_alias=n_alias),
        grid=(b, N_HEADS),
        in_specs=[col(COL_Q), col(COL_Q + 1), col(COL_Q + 2), col(COL_K), col(COL_V), col(COL_ZB),
                  small, small, small] + [any_spec] * n_alias,
        out_specs=[pl.BlockSpec((1, seq, LANES), lambda i, j: (i, 0, j)), kv_out, kv_out],
        out_shape=[jax.ShapeDtypeStruct((b, seq, BR_W), _BF16), kv_shape, kv_shape],
        input_output_aliases={} if first else {9: 1, 10: 2},
        scratch_shapes=[pltpu.VMEM((seq, LANES), _F32)] * 2
        + [pltpu.VMEM((QB, 2 * QB), _F32), pltpu.VMEM((QB, QB), _F32)]
        + [pltpu.VMEM((N_GROUPS, seq, LANES), _F32)] * 3,
        compiler_params=_params(("arbitrary", "arbitrary")),
        name="attn_prompt",
    )(h3, h3, h3, h3, h3, h3, qg, kg, inv, *(() if first else kv_bufs))


def _cache_shift_kernel(k_ref, kn_ref, v_ref, vn_ref, ok_ref, ov_ref, *, t_len):
    nj = k_ref.shape[2]
    keep = 16 - t_len
    last = pl.program_id(2) == pl.num_programs(2) - 1
    for src, nxt, dst in ((k_ref, kn_ref, ok_ref), (v_ref, vn_ref, ov_ref)):
        dst[0, 0, :, 0:keep] = src[0, 0, :, t_len:16]
        dst[0, 0, 0:nj - 1, keep:16] = src[0, 0, 1:nj, 0:t_len]
        dst[0, 0, nj - 1, keep:16] = jnp.where(last, 0.0, nxt[0, 0, 0, 0:t_len])


def _cache_shift(cache_k, cache_v, t_len, nj_blk=64):
    depth, b, n_rows = cache_k.shape[:3]
    nj = n_rows // 16
    assert t_len < 16 and nj % nj_blk == 0
    view = (depth, b, nj, 16, N_HEADS, LANES)
    main = pl.BlockSpec((1, 1, nj_blk, 16, N_HEADS, LANES), lambda l, i, c: (l, i, c, 0, 0, 0))
    nxt = pl.BlockSpec((1, 1, 1, 16, N_HEADS, LANES),
                       lambda l, i, c: (l, i, jnp.minimum((c + 1) * nj_blk, nj - 1), 0, 0, 0))
    kc, vc = cache_k.reshape(view), cache_v.reshape(view)
    return pl.pallas_call(
        functools.partial(_cache_shift_kernel, t_len=t_len),
        grid=(depth, b, nj // nj_blk),
        in_specs=[main, nxt, main, nxt],
        out_specs=[main, main],
        out_shape=[jax.ShapeDtypeStruct(view, _F32)] * 2,
        compiler_params=_params(("arbitrary", "arbitrary", "arbitrary")),
        name="cache_shift",
    )(kc, kc, vc, vc)


def _attn_sample_kernel(q0_ref, q1_ref, q2_ref, k_ref, v_ref, zb_ref, qg_ref, kg_ref, inv_ref, kc_ref, vc_ref,
                        kbuf_hbm, vbuf_hbm, yb_ref, ok_hbm, ov_hbm, kn_ref, vn_ref, sem, *, t_len, n_rows, layer):
    b = pl.program_id(0)
    shape = (t_len, N_HEADS, LANES)
    pos = (lax.broadcasted_iota(jnp.int32, shape, 0) + PAST_LEN).astype(_F32)
    lane = lax.broadcasted_iota(jnp.int32, shape, 2)
    cos_t, sin_t = _rope_tables(pos, lane, inv_ref[...].reshape(1, 1, LANES))

    def norm_rope(x, g_ref):
        y = _rms_heads(x, g_ref[...].reshape(1, 1, LANES))
        return y * cos_t + _rotate_half_lanes(y) * sin_t

    scale = HEAD_DIM ** -0.5
    k_new = norm_rope(k_ref[0], kg_ref)
    v_new = v_ref[0]
    kn_ref[...] = k_new
    vn_ref[...] = v_new
    nj = n_rows // 16
    tail = pl.ds(16 - t_len, t_len)
    copies = [pltpu.make_async_copy(kn_ref, ok_hbm.at[layer, b, nj - 1, tail], sem.at[0]),
              pltpu.make_async_copy(vn_ref, ov_hbm.at[layer, b, nj - 1, tail], sem.at[1])]
    for cp in copies:
        cp.start()

    qs = [norm_rope(q_ref[0], qg_ref) * scale for q_ref in (q0_ref, q1_ref, q2_ref)]

    for t in range(t_len):
        outs, lses = [], []
        for g, (w, d) in enumerate(GROUPS):
            q = qs[g][t]
            pieces = []
            if d == 1:
                lo = nj - QB // 16
                kc = kc_ref[0, 0, lo:nj].reshape(QB, N_HEADS, LANES)
                vc = vc_ref[0, 0, lo:nj].reshape(QB, N_HEADS, LANES)
                idx = lax.broadcasted_iota(jnp.int32, (QB, N_HEADS, 1), 0)
                pieces.append((kc, vc, idx >= t))
                pieces.append((k_new[:t + 1], v_new[:t + 1], None))
            else:
                per = 16 // d
                lo = nj - (w // d) // per
                for m in range(per):
                    pieces.append((kc_ref[0, 0, lo:nj, t + d * m], vc_ref[0, 0, lo:nj, t + d * m], None))
                pieces.append((k_new[t:t + 1], v_new[t:t + 1], None))
            scores = []
            for kk, vv, mask in pieces:
                s = jnp.sum(kk * q[None], axis=-1, keepdims=True)
                if mask is not None:
                    s = jnp.where(mask, s, NEG)
                scores.append(s)
            mx = functools.reduce(jnp.maximum, [jnp.max(s, axis=0) for s in scores])
            den = 0.0
            acc = 0.0
            for s, (kk, vv, mask) in zip(scores, pieces):
                p = jnp.exp(s - mx[None])
                den = den + jnp.sum(p, axis=0)
                acc = acc + jnp.sum(p * vv, axis=0)
            outs.append(acc / den)
            lses.append(mx + jnp.log(den))
        mx = jnp.maximum(jnp.maximum(lses[0], lses[1]), lses[2])
        ws = [jnp.exp(l - mx) for l in lses]
        o = (ws[0] * outs[0] + ws[1] * outs[1] + ws[2] * outs[2]) / (ws[0] + ws[1] + ws[2])
        yb_ref[0, t] = o * _silu(zb_ref[0, t])

    for cp in copies:
        cp.wait()


def _attn_sample(h4, qg, kg, inv, cache_k, cache_v, layer, kv_bufs):
    b, t_len = h4.shape[:2]
    n_rows = cache_k.shape[2]
    kc = cache_k.reshape(DEPTH, b, n_rows // 16, 16, N_HEADS, LANES)
    vc = cache_v.reshape(DEPTH, b, n_rows // 16, 16, N_HEADS, LANES)

    def col(cb):
        return pl.BlockSpec((1, t_len, N_HEADS, LANES), lambda i: (i, 0, cb, 0))

    small = pl.BlockSpec((1, LANES), lambda i: (0, 0))
    cache = pl.BlockSpec((1, 1, n_rows // 16, 16, N_HEADS, LANES), lambda i: (layer, i, 0, 0, 0, 0))
    any_spec = pl.BlockSpec(memory_space=pl.ANY)
    rows_shape = (t_len, N_HEADS, LANES)
    return pl.pallas_call(
        functools.partial(_attn_sample_kernel, t_len=t_len, n_rows=n_rows, layer=layer),
        grid=(b,),
        in_specs=[col(COL_Q), col(COL_Q + 1), col(COL_Q + 2), col(COL_K), col(COL_V), col(COL_ZB),
                  small, small, small, cache, cache, any_spec, any_spec],
        out_specs=[pl.BlockSpec((1, t_len, N_HEADS, LANES), lambda i: (i, 0, 0, 0)), any_spec, any_spec],
        out_shape=[jax.ShapeDtypeStruct((b, t_len, N_HEADS, LANES), _F32),
                   jax.ShapeDtypeStruct(kc.shape, _F32), jax.ShapeDtypeStruct(vc.shape, _F32)],
        input_output_aliases={11: 1, 12: 2},
        scratch_shapes=[pltpu.VMEM(rows_shape, _F32)] * 2 + [pltpu.SemaphoreType.DMA((2,))],
        compiler_params=_params(("arbitrary",)),
        name="attn_sample",
    )(h4, h4, h4, h4, h4, h4, qg, kg, inv, kc, vc, *kv_bufs)


def _branch_kernel(va_ref, ca_ref, ba_ref, za_ref, uc_ref, zc_ref, ga_ref, gb_ref, zd_ref,
                   sta_ref, stc_ref, std_ref, aw_ref, pw_ref, cs_ref, dw_ref, db_ref, lg_ref, lb_ref,
                   ya_ref, yc_ref, yd_ref, na_ref, nc_ref, nd_ref,
                   eas_ref, ec_ref, eds_ref, pp_ref, *, tm, pos0):
    i = pl.program_id(1)
    ea_ref = eas_ref.at[0]
    ed_ref = eds_ref.at[0]
    a_offs = [HALO - (A_CONV - 1) + k for k in range(A_CONV)]
    a_shifts = sorted({off % 8 for off in a_offs} | {0})
    assert len(a_shifts) <= eas_ref.shape[0]

    @pl.when(i == 0)
    def _():
        ea_ref[0:HALO, :] = sta_ref[0]
        ec_ref[0:HALO, :] = stc_ref[0]
        ed_ref[0:HALO, :] = std_ref[0]
        pp_ref[...] = jnp.zeros_like(pp_ref)

    @pl.when(i > 0)
    def _():
        for e_ref in (ea_ref, ec_ref, ed_ref):
            e_ref[0:HALO, :] = e_ref[tm:tm + HALO, :]

    new = pl.ds(HALO, tm)
    ea_ref[new, :] = ca_ref[0] * va_ref[0]
    ec_ref[new, :] = uc_ref[0]
    ed_ref[new, :] = ga_ref[0] * _sigmoid(gb_ref[0])
    n_shift = HALO + tm - 8
    for r in range(1, 8):
        eds_ref[r, 0:n_shift, :] = ed_ref[pl.ds(r, n_shift), :]
    for slot, r in enumerate(a_shifts):
        if r:
            eas_ref[slot, 0:n_shift, :] = ea_ref[pl.ds(r, n_shift), :]

    pos = pos0 + i * tm + lax.broadcasted_iota(jnp.int32, (tm, C_GROUP), 0)
    for g, w in enumerate(POOL_WINDOWS):
        cols = slice(g * C_GROUP, (g + 1) * C_GROUP)
        tok = ec_ref[new, cols]
        s = tok
        for j in range(1, w):
            s = s + ec_ref[pl.ds(HALO - j, tm), cols]
        cnt = jnp.minimum(pos + 1, w).astype(_F32)
        pp_ref[0:tm, :] = s / cnt - tok
        y = jnp.dot(pp_ref[...].astype(_BF16), pw_ref[g], preferred_element_type=_F32)[0:tm]
        yc_ref[0, :, cols] = ((y * cs_ref[:, cols]) * _silu(zc_ref[0, :, cols])).astype(_BF16)

    rc = min(tm, 32)

    def conv_rows(w_ref, e_ref, slot_of, offs, r0):
        acc = None
        for k, off in enumerate(offs):
            term = w_ref[k:k + 1, :] * e_ref[slot_of(off % 8), pl.ds(off // 8 * 8 + r0, rc), :]
            acc = term if acc is None else acc + term
        return acc

    def chunk(r0):
        rows = pl.ds(r0, rc)
        conv = conv_rows(aw_ref, eas_ref, a_shifts.index, a_offs, r0)
        ya_ref[0, rows, :] = (ba_ref[0, rows, :] * conv * _silu(za_ref[0, rows, :])).astype(_BF16)
        d_offs = [HALO - (D_CONV - 1) + k for k in range(D_CONV)]
        x = conv_rows(dw_ref, eds_ref, lambda r: r, d_offs, r0) + db_ref[...]
        mu = jnp.mean(x, axis=-1, keepdims=True)
        xc = x - mu
        var = jnp.mean(xc * xc, axis=-1, keepdims=True)
        y = xc * lax.rsqrt(var + EPS) * lg_ref[...] + lb_ref[...]
        yd_ref[0, rows, :] = (_silu(y) * _silu(zd_ref[0, rows, :])).astype(_BF16)

    if tm == rc:
        chunk(0)
    else:
        def body(c, carry):
            chunk(pl.multiple_of(c * rc, rc))
            return carry
        lax.fori_loop(0, tm // rc, body, 0)

    @pl.when(i == pl.num_programs(1) - 1)
    def _():
        na_ref[0] = ea_ref[pl.ds(HALO + tm - (A_CONV - 1), A_CONV - 1), :]
        nc_ref[0] = ec_ref[pl.ds(HALO + tm - POOL_PAST, POOL_PAST), :]
        nd_ref[0] = ed_ref[pl.ds(HALO + tm - (D_CONV - 1), D_CONV - 1), :]


def _branches(h3, states, wl, pos0, tm):
    b, t_len, _ = h3.shape

    def col(cb):
        return pl.BlockSpec((1, tm, BR_W), lambda i, j: (i, j, cb))

    def full(shape):
        return pl.BlockSpec(shape, lambda i, j: (0,) * len(shape))

    state = pl.BlockSpec((1, HALO, BR_W), lambda i, j: (i, 0, 0))
    y_out = pl.BlockSpec((1, tm, BR_W), lambda i, j: (i, j, 0))

    def st_out(n):
        return pl.BlockSpec((1, n, BR_W), lambda i, j: (i, 0, 0))

    tmc = max(tm, 8)
    n_pool = len(POOL_WINDOWS)
    return pl.pallas_call(
        functools.partial(_branch_kernel, tm=tm, pos0=pos0),
        grid=(b, t_len // tm),
        in_specs=[col(COL_VA), col(COL_CA), col(COL_BA), col(COL_ZA), col(COL_UC), col(COL_ZC),
                  col(COL_GA), col(COL_GB), col(COL_ZD), state, state, state,
                  full((A_CONV, BR_W)), full((n_pool, C_GROUP, C_GROUP)), full((1, BR_W)),
                  full((D_CONV, BR_W)), full((1, BR_W)), full((1, BR_W)), full((1, BR_W))],
        out_specs=[y_out, y_out, y_out, st_out(A_CONV - 1), st_out(POOL_PAST), st_out(D_CONV - 1)],
        out_shape=[jax.ShapeDtypeStruct((b, t_len, BR_W), _BF16)] * 3
        + [jax.ShapeDtypeStruct((b, n, BR_W), _F32) for n in (A_CONV - 1, POOL_PAST, D_CONV - 1)],
        scratch_shapes=[pltpu.VMEM((A_CONV, HALO + tmc, BR_W), _F32), pltpu.VMEM((HALO + tmc, BR_W), _F32),
                        pltpu.VMEM((8, HALO + tmc, BR_W), _F32), pltpu.VMEM((tmc, C_GROUP), _F32)],
        compiler_params=_params(("arbitrary", "arbitrary")),
        name="branches",
    )(*([h3] * 9), *states, wl["a_conv_w"], wl["c_pool_w"], wl["c_scale"], wl["d_conv_w"],
      wl["d_conv_b"], wl["d_ln_g"], wl["d_ln_b"])


def _merge_kernel(ya_ref, yb_ref, yc_ref, yd_ref, wa_ref, wb_ref, wc_ref, wd_ref,
                  g0_ref, g1_ref, g2_ref, g3_ref, o_ref):
    acc = None
    for y_ref, w_ref, g_ref in ((ya_ref, wa_ref, g0_ref), (yb_ref, wb_ref, g1_ref),
                                (yc_ref, wc_ref, g2_ref), (yd_ref, wd_ref, g3_ref)):
        term = _sigmoid(g_ref[...]) * jnp.dot(y_ref[...].astype(_BF16), w_ref[...],
                                                    preferred_element_type=_F32)
        acc = term if acc is None else acc + term
    o_ref[...] = acc.astype(_BF16)


def _merge(ys, w_brs, h2, layer, tm, tn=512):
    m = h2.shape[0]
    gate0 = COL_GATE * BR_W // tn
    y_spec = pl.BlockSpec((tm, BR_W), lambda i, j: (i, 0))
    w_spec = pl.BlockSpec((None, BR_W, tn), lambda i, j: (layer, 0, j))

    def gate(n):
        return pl.BlockSpec((tm, tn), lambda i, j: (i, gate0 + n * (D_MODEL // tn) + j))

    return pl.pallas_call(
        _merge_kernel,
        grid=(m // tm, D_MODEL // tn),
        in_specs=[y_spec] * 4 + [w_spec] * 4 + [gate(n) for n in range(N_BRANCH)],
        out_specs=pl.BlockSpec((tm, tn), lambda i, j: (i, j)),
        out_shape=jax.ShapeDtypeStruct((m, D_MODEL), _BF16),
        compiler_params=_params(("parallel", "arbitrary")),
        name="merge",
    )(*ys, *w_brs, h2, h2, h2, h2)


def _outproj_kernel(m_ref, w_ref, x_ref, o_ref):
    o_ref[...] = x_ref[...] + jnp.dot(m_ref[...], w_ref[...], preferred_element_type=_F32)


def _outproj(merged, w_out, x2, layer, tm, tn=1024):
    m = x2.shape[0]
    return pl.pallas_call(
        _outproj_kernel,
        grid=(m // tm, D_MODEL // tn),
        in_specs=[pl.BlockSpec((tm, D_MODEL), lambda i, j: (i, 0)),
                  pl.BlockSpec((None, D_MODEL, tn), lambda i, j: (layer, 0, j)),
                  pl.BlockSpec((tm, tn), lambda i, j: (i, j))],
        out_specs=pl.BlockSpec((tm, tn), lambda i, j: (i, j)),
        out_shape=jax.ShapeDtypeStruct((m, D_MODEL), _F32),
        compiler_params=_params(("parallel", "arbitrary")),
        name="outproj",
    )(merged, w_out, x2)


def _layer(x3, h2, layer, past, wl, big, kv_bufs):
    b, t_len, _ = x3.shape
    m = b * t_len
    x2 = x3.reshape(m, D_MODEL)
    tm_mat = min(m, 1024)
    h3 = h2.reshape(b, t_len, N_IN)

    if past is None:
        yb, k_buf, v_buf = _attn_prompt(h3, wl["q_norm_g"], wl["k_norm_g"], wl["inv"], layer, kv_bufs)
        states = [jnp.zeros((b, HALO, BR_W), _F32)] * 3
        pos0, tm_br = 0, 256
    else:
        h4 = h2.reshape(b, t_len, N_IN // LANES, LANES)
        yb, k_buf, v_buf = _attn_sample(h4, wl["q_norm_g"], wl["k_norm_g"], wl["inv"], past[0], past[1],
                                        layer, kv_bufs)
        states = [jnp.pad(s[layer], ((0, 0), (HALO - s.shape[2], 0), (0, 0))) for s in past[2:]]
        pos0, tm_br = PAST_LEN, t_len
    yb = yb.reshape(m, BR_W)

    ya, yc, yd, new_a, new_c, new_d = _branches(h3, states, wl, pos0, tm_br)
    ys = (ya.reshape(m, BR_W), yb, yc.reshape(m, BR_W), yd.reshape(m, BR_W))
    merged = _merge(ys, big["w_br"], h2, layer, tm_mat)
    out = _outproj(merged, big["w_out"], x2, layer, tm_mat)
    return out.reshape(b, t_len, D_MODEL), (k_buf, v_buf), (new_a, new_c, new_d)


def kernel(x_prompt, x_sample, cache_attn_k, cache_attn_v, state_conv_a, state_pool_c, state_conv_d,
           norm_g, w_in, q_norm_g, k_norm_g, a_conv_w, c_pool_w, c_scale, d_conv_w, d_conv_b,
           d_ln_g, d_ln_b, w_br_a, w_br_b, w_br_c, w_br_d, w_out):
    half = ROT_DIM // 2
    inv = ROPE_THETA ** (-(jnp.arange(half, dtype=_F32) / half))
    inv = jnp.concatenate([inv, inv, jnp.zeros((LANES - ROT_DIM,), _F32)])[None, :]
    big = {"w_out": w_out.astype(_BF16), "w_br": tuple(w.astype(_BF16) for w in (w_br_a, w_br_b, w_br_c, w_br_d))}
    pool_w = c_pool_w.astype(_BF16)
    past = (cache_attn_k, cache_attn_v, state_conv_a, state_pool_c, state_conv_d)

    hp, hs = x_prompt, x_sample
    kv_p = None
    kv_s = _cache_shift(cache_attn_k, cache_attn_v, x_sample.shape[1])
    st_p = [[] for _ in range(3)]
    st_s = [[] for _ in range(3)]
    for l in range(DEPTH):
        wl = {"norm_g": norm_g[l][None], "q_norm_g": q_norm_g[l][None], "k_norm_g": k_norm_g[l][None],
              "inv": inv, "a_conv_w": a_conv_w[l], "c_pool_w": pool_w[l], "c_scale": c_scale[l][None],
              "d_conv_w": d_conv_w[l], "d_conv_b": d_conv_b[l][None], "d_ln_g": d_ln_g[l][None],
              "d_ln_b": d_ln_b[l][None]}
        h_s, w_in_bf16 = _inproj_cast(hs.reshape(-1, D_MODEL), wl["norm_g"], w_in, l)
        h_p = _inproj(hp.reshape(-1, D_MODEL), wl["norm_g"], w_in_bf16, 1024)
        hp, kv_p, sp = _layer(hp, h_p, l, None, wl, big, kv_p)
        hs, kv_s, ss = _layer(hs, h_s, l, past, wl, big, kv_s)
        for n in range(3):
            st_p[n].append(sp[n])
            st_s[n].append(ss[n])
    b, seq = x_prompt.shape[:2]
    kv_p = [a.reshape(DEPTH, b, seq, N_HEADS, HEAD_DIM) for a in kv_p]
    kv_s = [a.reshape(cache_attn_k.shape) for a in kv_s]
    return (hp, hs, *kv_p, *(jnp.stack(s, axis=0) for s in st_p),
            *kv_s, *(jnp.stack(s, axis=0) for s in st_s))
```

```python
import functools

import jax
import jax.numpy as jnp
from jax import lax
from jax.experimental import pallas as pl
from jax.experimental.pallas import tpu as pltpu

D_MODEL = 2048
DEPTH = 2
PAST_LEN = 16384
BR_W = D_MODEL // 2
N_BRANCH = 4
A_CONV = 3
HEAD_DIM = 128
N_HEADS = BR_W // HEAD_DIM
GROUPS = ((128, 1), (512, 4), (2048, 16))
N_GROUPS = len(GROUPS)
ROT_DIM = HEAD_DIM // 4
ROPE_THETA = 500000.0
POOL_WINDOWS = (2, 4, 8, 16)
C_GROUP = BR_W // len(POOL_WINDOWS)
POOL_PAST = POOL_WINDOWS[-1] - 1
D_CONV = 31
EPS = 1e-6
N_IN = 4 * BR_W + (N_GROUPS + 3) * BR_W + 2 * BR_W + 3 * BR_W + N_BRANCH * D_MODEL

COL_VA, COL_CA, COL_BA, COL_ZA = 0, 1, 2, 3
COL_Q, COL_K, COL_V, COL_ZB = 4, 7, 8, 9
COL_UC, COL_ZC = 10, 11
COL_GA, COL_GB, COL_ZD = 12, 13, 14
COL_GATE = 15

LANES = 128
QB = 128
SAMPLE_RECENT = 512
HALO = 32
NEG = -1e30
VMEM_LIMIT = 56 * 1024 * 1024

_BF16 = jnp.bfloat16
_F32 = jnp.float32


def _sigmoid(x):
    return 0.5 * jnp.tanh(0.5 * x) + 0.5


def _silu(x):
    return x * _sigmoid(x)


def _params(sem):
    return pltpu.CompilerParams(dimension_semantics=sem, vmem_limit_bytes=VMEM_LIMIT)


def _inproj_kernel(x_ref, g_ref, w_ref, h_ref, xn_ref):
    @pl.when(pl.program_id(1) == 0)
    def _():
        x = x_ref[...]
        y = x * lax.rsqrt(jnp.mean(x * x, axis=-1, keepdims=True) + EPS)
        xn_ref[...] = (y * g_ref[...]).astype(_BF16)

    h_ref[...] = jnp.dot(xn_ref[...], w_ref[...], preferred_element_type=_F32)


def _inproj(x2, g, w_bf16, tm, tn=1024):
    m = x2.shape[0]
    return pl.pallas_call(
        _inproj_kernel,
        grid=(m // tm, N_IN // tn),
        in_specs=[
            pl.BlockSpec((tm, D_MODEL), lambda i, j: (i, 0)),
            pl.BlockSpec((1, D_MODEL), lambda i, j: (0, 0)),
            pl.BlockSpec((D_MODEL, tn), lambda i, j: (0, j)),
        ],
        out_specs=pl.BlockSpec((tm, tn), lambda i, j: (i, j)),
        out_shape=jax.ShapeDtypeStruct((m, N_IN), _F32),
        scratch_shapes=[pltpu.VMEM((tm, D_MODEL), _BF16)],
        compiler_params=_params(("parallel", "arbitrary")),
        name="inproj",
    )(x2, g, w_bf16)


def _inproj_cast_kernel(x_ref, g_ref, w_ref, h_ref, wb_ref, xn_ref):
    @pl.when(pl.program_id(0) == 0)
    def _():
        x = x_ref[...]
        y = x * lax.rsqrt(jnp.mean(x * x, axis=-1, keepdims=True) + EPS)
        xn_ref[...] = (y * g_ref[...]).astype(_BF16)

    wb = w_ref[...].astype(_BF16)
    wb_ref[...] = wb
    h_ref[...] = jnp.dot(xn_ref[...], wb, preferred_element_type=_F32)


def _inproj_cast(x2, g, w_in, layer, tn=1024):
    m = x2.shape[0]
    return pl.pallas_call(
        _inproj_cast_kernel,
        grid=(N_IN // tn,),
        in_specs=[
            pl.BlockSpec((m, D_MODEL), lambda j: (0, 0)),
            pl.BlockSpec((1, D_MODEL), lambda j: (0, 0)),
            pl.BlockSpec((None, D_MODEL, tn), lambda j: (layer, 0, j)),
        ],
        out_specs=[pl.BlockSpec((m, tn), lambda j: (0, j)), pl.BlockSpec((D_MODEL, tn), lambda j: (0, j))],
        out_shape=[jax.ShapeDtypeStruct((m, N_IN), _F32), jax.ShapeDtypeStruct((D_MODEL, N_IN), _BF16)],
        scratch_shapes=[pltpu.VMEM((m, D_MODEL), _BF16)],
        compiler_params=_params(("arbitrary",)),
        name="inproj_cast",
    )(x2, g, w_in)


def _rope_tables(pos, lane, inv):
    ang = pos * inv
    cos_t = jnp.where(lane < ROT_DIM, jnp.cos(ang), 1.0)
    sin_t = jnp.where(lane < ROT_DIM, jnp.sin(ang), 0.0)
    return cos_t, sin_t


def _rotate_half_matrix():
    half = ROT_DIM // 2
    m = lax.broadcasted_iota(jnp.int32, (LANES, LANES), 0)
    l = lax.broadcasted_iota(jnp.int32, (LANES, LANES), 1)
    return jnp.where(m == l + half, jnp.where(l < half, -1.0, 0.0),
                     jnp.where(m == l - half, jnp.where(l < ROT_DIM, 1.0, 0.0), 0.0))


def _rotate_half_lanes(x):
    ax = x.ndim - 1
    half = ROT_DIM // 2
    lane = lax.broadcasted_iota(jnp.int32, x.shape, ax)
    return jnp.where(lane < half, -pltpu.roll(x, LANES - half, ax), pltpu.roll(x, half, ax))


def _dot_hi_lo(x, w2):
    hi = x.astype(_BF16)
    lo = (x - hi.astype(_F32)).astype(_BF16)
    return jnp.dot(jnp.concatenate([hi, lo], axis=1), w2, preferred_element_type=_F32)


def _rms_heads(x, g):
    return x * lax.rsqrt(jnp.mean(x * x, axis=-1, keepdims=True) + EPS) * g


def _attn_block(q, k, v, bias):
    s = lax.dot_general(q.astype(_BF16), k.astype(_BF16), (((1,), (1,)), ((), ())),
                        preferred_element_type=_F32) + bias
    m = jnp.max(s, axis=-1, keepdims=True)
    p = jnp.exp(s - m).astype(_BF16)
    v1 = jnp.concatenate([v.astype(_BF16), jnp.ones(v.shape, _BF16)], axis=1)
    acc = jnp.dot(p, v1, preferred_element_type=_F32)
    l = acc[:, LANES:]
    return acc[:, :LANES] * (1.0 / l), m + jnp.log(l)


def _attn_prompt_kernel(*refs, seq, n_alias):
    q0_ref, q1_ref, q2_ref, k_ref, v_ref, zb_ref, qg_ref, kg_ref, inv_ref = refs[:9]
    yb_ref, ko_ref, vo_ref = refs[9 + n_alias:12 + n_alias]
    cos_ref, sin_ref, band_ref, causal_ref, qs_ref, og_ref, lg_ref = refs[12 + n_alias:]
    rows = 256
    n_chunks = seq // rows

    @pl.when((pl.program_id(0) == 0) & (pl.program_id(1) == 0))
    def _():
        def body(c, carry):
            r0 = pl.multiple_of(c * rows, rows)
            pos = (lax.broadcasted_iota(jnp.int32, (rows, LANES), 0) + r0).astype(_F32)
            lane = lax.broadcasted_iota(jnp.int32, (rows, LANES), 1)
            cos_t, sin_t = _rope_tables(pos, lane, inv_ref[...])
            cos_ref[pl.ds(r0, rows), :] = cos_t
            sin_ref[pl.ds(r0, rows), :] = sin_t
            return carry
        lax.fori_loop(0, n_chunks, body, 0)
        ri = lax.broadcasted_iota(jnp.int32, (QB, 2 * QB), 0)
        ci = lax.broadcasted_iota(jnp.int32, (QB, 2 * QB), 1)
        band_ref[...] = jnp.where(ci < ri, NEG, jnp.where(ci > ri + QB, NEG, 0.0))
        rq = lax.broadcasted_iota(jnp.int32, (QB, QB), 0)
        cq = lax.broadcasted_iota(jnp.int32, (QB, QB), 1)
        causal_ref[...] = jnp.where(cq > rq, NEG, 0.0)

    scale = HEAD_DIM ** -0.5
    rot_m = _rotate_half_matrix().astype(_BF16)
    rot2 = jnp.concatenate([rot_m, rot_m], axis=0)
    ones2 = jnp.ones((2 * LANES, LANES), _BF16)

    def norm_rope(x, g, cos_t, sin_t):
        ssq = _dot_hi_lo(x * x, ones2)
        y = x * lax.rsqrt(ssq * (1.0 / HEAD_DIM) + EPS) * g
        return y * cos_t + _dot_hi_lo(y, rot2) * sin_t

    def prep(c, carry):
        r0 = pl.multiple_of(c * rows, rows)
        sl = pl.ds(r0, rows)
        cos_t, sin_t = cos_ref[sl, :], sin_ref[sl, :]
        for g, q_ref in enumerate((q0_ref, q1_ref, q2_ref)):
            qs_ref[g, sl, :] = norm_rope(q_ref[0, sl, :], qg_ref[...], cos_t, sin_t) * scale
        ko_ref[0, 0, sl, :] = norm_rope(k_ref[0, sl, :], kg_ref[...], cos_t, sin_t)
        vo_ref[0, 0, sl, :] = v_ref[0, sl, :]
        for l2 in range(1, ko_ref.shape[0]):
            ko_ref[l2, 0, sl, :] = jnp.zeros((rows, LANES), _F32)
            vo_ref[l2, 0, sl, :] = jnp.zeros((rows, LANES), _F32)
        return carry
    lax.fori_loop(0, n_chunks, prep, 0, unroll=True)

    def rows_of(start, n, d):
        return pl.ds(start, n) if d == 1 else pl.ds(start, n, stride=d)

    for g, (w, d) in enumerate(GROUPS):
        assert w // d == QB
        n_blk = seq // (d * QB)
        for r in range(d):
            for blk in range(n_blk):
                q = qs_ref[g, rows_of(r + d * QB * blk, QB, d), :]
                if blk == 0:
                    ksl, bias = rows_of(r, QB, d), causal_ref[...]
                else:
                    ksl, bias = rows_of(r + d * QB * (blk - 1), 2 * QB, d), band_ref[...]
                o, lse = _attn_block(q, ko_ref[0, 0, ksl, :], v_ref[0, ksl, :], bias)
                osl = rows_of(r + d * QB * blk, QB, d)
                og_ref[g, osl, :] = o
                lg_ref[g, osl, :] = lse

    def merge(c, carry):
        r0 = pl.multiple_of(c * rows, rows)
        sl = pl.ds(r0, rows)
        l0, l1, l2 = lg_ref[0, sl, :], lg_ref[1, sl, :], lg_ref[2, sl, :]
        mx = jnp.maximum(jnp.maximum(l0, l1), l2)
        w0, w1, w2 = jnp.exp(l0 - mx), jnp.exp(l1 - mx), jnp.exp(l2 - mx)
        o = (w0 * og_ref[0, sl, :] + w1 * og_ref[1, sl, :] + w2 * og_ref[2, sl, :]) / (w0 + w1 + w2)
        yb_ref[0, sl, :] = (o * _silu(zb_ref[0, sl, :])).astype(_BF16)
        return carry
    lax.fori_loop(0, n_chunks, merge, 0)


def _attn_prompt(h3, qg, kg, inv, layer, kv_bufs):
    b, seq, _ = h3.shape
    hb = BR_W // LANES
    first = kv_bufs is None
    assert first == (layer == 0)

    def col(cb):
        return pl.BlockSpec((1, seq, LANES), lambda i, j: (i, 0, cb * hb + j))

    small = pl.BlockSpec((1, LANES), lambda i, j: (0, 0))
    any_spec = pl.BlockSpec(memory_space=pl.ANY)
    kv_out = pl.BlockSpec((DEPTH if first else 1, 1, seq, LANES), lambda i, j: (layer, i, 0, j))
    kv_shape = jax.ShapeDtypeStruct((DEPTH, b, seq, BR_W), _F32)
    n_alias = 0 if first else 2
    return pl.pallas_call(
        functools.partial(_attn_prompt_kernel, seq=seq, n_alias=n_alias),
        grid=(b, N_HEADS),
        in_specs=[col(COL_Q), col(COL_Q + 1), col(COL_Q + 2), col(COL_K), col(COL_V), col(COL_ZB),
                  small, small, small] + [any_spec] * n_alias,
        out_specs=[pl.BlockSpec((1, seq, LANES), lambda i, j: (i, 0, j)), kv_out, kv_out],
        out_shape=[jax.ShapeDtypeStruct((b, seq, BR_W), _BF16), kv_shape, kv_shape],
        input_output_aliases={} if first else {9: 1, 10: 2},
        scratch_shapes=[pltpu.VMEM((seq, LANES), _F32)] * 2
        + [pltpu.VMEM((QB, 2 * QB), _F32), pltpu.VMEM((QB, QB), _F32)]
        + [pltpu.VMEM((N_GROUPS, seq, LANES), _F32)] * 3,
        compiler_params=_params(("arbitrary", "arbitrary")),
        name="attn_prompt",
    )(h3, h3, h3, h3, h3, h3, qg, kg, inv, *(() if first else kv_bufs))


def _cache_shift_kernel(k_ref, kn_ref, v_ref, vn_ref, ok_ref, ov_ref, *, t_len):
    nj = k_ref.shape[2]
    keep = 16 - t_len
    last = pl.program_id(2) == pl.num_programs(2) - 1
    for src, nxt, dst in ((k_ref, kn_ref, ok_ref), (v_ref, vn_ref, ov_ref)):
        dst[0, 0, :, 0:keep] = src[0, 0, :, t_len:16]
        dst[0, 0, 0:nj - 1, keep:16] = src[0, 0, 1:nj, 0:t_len]
        dst[0, 0, nj - 1, keep:16] = jnp.where(last, 0.0, nxt[0, 0, 0, 0:t_len])


def _cache_shift(cache_k, cache_v, t_len, nj_blk=64):
    depth, b, n_rows = cache_k.shape[:3]
    nj = n_rows // 16
    assert t_len < 16 and nj % nj_blk == 0
    view = (depth, b, nj, 16, N_HEADS, LANES)
    main = pl.BlockSpec((1, 1, nj_blk, 16, N_HEADS, LANES), lambda l, i, c: (l, i, c, 0, 0, 0))
    nxt = pl.BlockSpec((1, 1, 1, 16, N_HEADS, LANES),
                       lambda l, i, c: (l, i, jnp.minimum((c + 1) * nj_blk, nj - 1), 0, 0, 0))
    kc, vc = cache_k.reshape(view), cache_v.reshape(view)
    return pl.pallas_call(
        functools.partial(_cache_shift_kernel, t_len=t_len),
        grid=(depth, b, nj // nj_blk),
        in_specs=[main, nxt, main, nxt],
        out_specs=[main, main],
        out_shape=[jax.ShapeDtypeStruct(view, _F32)] * 2,
        compiler_params=_params(("arbitrary", "arbitrary", "arbitrary")),
        name="cache_shift",
    )(kc, kc, vc, vc)


def _attn_sample_kernel(q0_ref, q1_ref, q2_ref, k_ref, v_ref, zb_ref, qg_ref, kg_ref, inv_ref,
                        kres_ref, krec_ref, vres_ref, vrec_ref, kbuf_hbm, vbuf_hbm,
                        yb_ref, ok_hbm, ov_hbm, kn_ref, vn_ref, sem, *, t_len, n_rows, layer):
    b = pl.program_id(0)
    shape = (t_len, N_HEADS, LANES)
    pos = (lax.broadcasted_iota(jnp.int32, shape, 0) + PAST_LEN).astype(_F32)
    lane = lax.broadcasted_iota(jnp.int32, shape, 2)
    cos_t, sin_t = _rope_tables(pos, lane, inv_ref[...].reshape(1, 1, LANES))

    def norm_rope(x, g_ref):
        y = _rms_heads(x, g_ref[...].reshape(1, 1, LANES))
        return y * cos_t + _rotate_half_lanes(y) * sin_t

    scale = HEAD_DIM ** -0.5
    k_new = norm_rope(k_ref[0], kg_ref)
    v_new = v_ref[0]
    kn_ref[...] = k_new
    vn_ref[...] = v_new
    nj = n_rows // 16
    tail = pl.ds(16 - t_len, t_len)
    copies = [pltpu.make_async_copy(kn_ref, ok_hbm.at[layer, b, nj - 1, tail], sem.at[0]),
              pltpu.make_async_copy(vn_ref, ov_hbm.at[layer, b, nj - 1, tail], sem.at[1])]
    for cp in copies:
        cp.start()

    qs = [norm_rope(q_ref[0], qg_ref) * scale for q_ref in (q0_ref, q1_ref, q2_ref)]

    for t in range(t_len):
        outs, lses = [], []
        for g, (w, d) in enumerate(GROUPS):
            q = qs[g][t]
            pieces = []
            nrec = krec_ref.shape[2]
            if d == 1:
                lo = nrec - QB // 16
                kc = krec_ref[0, 0, lo:nrec].reshape(QB, N_HEADS, LANES)
                vc = vrec_ref[0, 0, lo:nrec].reshape(QB, N_HEADS, LANES)
                idx = lax.broadcasted_iota(jnp.int32, (QB, N_HEADS, 1), 0)
                pieces.append((kc, vc, idx >= t))
                pieces.append((k_new[:t + 1], v_new[:t + 1], None))
            elif d < 16:
                per = 16 // d
                lo = nrec - (w // d) // per
                for m in range(per):
                    pieces.append((krec_ref[0, 0, lo:nrec, t + d * m], vrec_ref[0, 0, lo:nrec, t + d * m], None))
                pieces.append((k_new[t:t + 1], v_new[t:t + 1], None))
            else:
                assert d == 16
                lo = nj - w // 16
                pieces.append((kres_ref[0, 0, lo:nj, t], vres_ref[0, 0, lo:nj, t], None))
                pieces.append((k_new[t:t + 1], v_new[t:t + 1], None))
            scores = []
            for kk, vv, mask in pieces:
                s = jnp.sum(kk * q[None], axis=-1, keepdims=True)
                if mask is not None:
                    s = jnp.where(mask, s, NEG)
                scores.append(s)
            mx = functools.reduce(jnp.maximum, [jnp.max(s, axis=0) for s in scores])
            den = 0.0
            acc = 0.0
            for s, (kk, vv, mask) in zip(scores, pieces):
                p = jnp.exp(s - mx[None])
                den = den + jnp.sum(p, axis=0)
                acc = acc + jnp.sum(p * vv, axis=0)
            outs.append(acc / den)
            lses.append(mx + jnp.log(den))
        mx = jnp.maximum(jnp.maximum(lses[0], lses[1]), lses[2])
        ws = [jnp.exp(l - mx) for l in lses]
        o = (ws[0] * outs[0] + ws[1] * outs[1] + ws[2] * outs[2]) / (ws[0] + ws[1] + ws[2])
        yb_ref[0, t] = o * _silu(zb_ref[0, t])

    for cp in copies:
        cp.wait()


def _attn_sample(h4, qg, kg, inv, cache_k, cache_v, layer, kv_bufs):
    b, t_len = h4.shape[:2]
    n_rows = cache_k.shape[2]
    kc = cache_k.reshape(DEPTH, b, n_rows // 16, 16, N_HEADS, LANES)
    vc = cache_v.reshape(DEPTH, b, n_rows // 16, 16, N_HEADS, LANES)

    def col(cb):
        return pl.BlockSpec((1, t_len, N_HEADS, LANES), lambda i: (i, 0, cb, 0))

    small = pl.BlockSpec((1, LANES), lambda i: (0, 0))
    nj = n_rows // 16
    nrec = SAMPLE_RECENT // 16
    assert nj % nrec == 0 and 16 % t_len == 0 and all(w <= SAMPLE_RECENT for w, d in GROUPS if d < 16)
    resid = pl.BlockSpec((1, 1, nj, t_len, N_HEADS, LANES), lambda i: (layer, i, 0, 0, 0, 0))
    recent = pl.BlockSpec((1, 1, nrec, 16, N_HEADS, LANES), lambda i: (layer, i, nj // nrec - 1, 0, 0, 0))
    any_spec = pl.BlockSpec(memory_space=pl.ANY)
    rows_shape = (t_len, N_HEADS, LANES)
    return pl.pallas_call(
        functools.partial(_attn_sample_kernel, t_len=t_len, n_rows=n_rows, layer=layer),
        grid=(b,),
        in_specs=[col(COL_Q), col(COL_Q + 1), col(COL_Q + 2), col(COL_K), col(COL_V), col(COL_ZB),
                  small, small, small, resid, recent, resid, recent, any_spec, any_spec],
        out_specs=[pl.BlockSpec((1, t_len, N_HEADS, LANES), lambda i: (i, 0, 0, 0)), any_spec, any_spec],
        out_shape=[jax.ShapeDtypeStruct((b, t_len, N_HEADS, LANES), _F32),
                   jax.ShapeDtypeStruct(kc.shape, _F32), jax.ShapeDtypeStruct(vc.shape, _F32)],
        input_output_aliases={13: 1, 14: 2},
        scratch_shapes=[pltpu.VMEM(rows_shape, _F32)] * 2 + [pltpu.SemaphoreType.DMA((2,))],
        compiler_params=_params(("arbitrary",)),
        name="attn_sample",
    )(h4, h4, h4, h4, h4, h4, qg, kg, inv, kc, kc, vc, vc, *kv_bufs)


def _branch_kernel(va_ref, ca_ref, ba_ref, za_ref, uc_ref, zc_ref, ga_ref, gb_ref, zd_ref,
                   sta_ref, stc_ref, std_ref, aw_ref, pw_ref, cs_ref, dw_ref, db_ref, lg_ref, lb_ref,
                   ya_ref, yc_ref, yd_ref, na_ref, nc_ref, nd_ref,
                   eas_ref, ec_ref, eds_ref, pp_ref, *, tm, pos0):
    i = pl.program_id(1)
    ea_ref = eas_ref.at[0]
    ed_ref = eds_ref.at[0]
    a_offs = [HALO - (A_CONV - 1) + k for k in range(A_CONV)]
    a_shifts = sorted({off % 8 for off in a_offs} | {0})
    assert len(a_shifts) <= eas_ref.shape[0]

    @pl.when(i == 0)
    def _():
        ea_ref[0:HALO, :] = sta_ref[0]
        ec_ref[0:HALO, :] = stc_ref[0]
        ed_ref[0:HALO, :] = std_ref[0]
        pp_ref[...] = jnp.zeros_like(pp_ref)

    @pl.when(i > 0)
    def _():
        for e_ref in (ea_ref, ec_ref, ed_ref):
            e_ref[0:HALO, :] = e_ref[tm:tm + HALO, :]

    new = pl.ds(HALO, tm)
    ea_ref[new, :] = ca_ref[0] * va_ref[0]
    ec_ref[new, :] = uc_ref[0]
    ed_ref[new, :] = ga_ref[0] * _sigmoid(gb_ref[0])
    n_shift = HALO + tm - 8
    for r in range(1, 8):
        eds_ref[r, 0:n_shift, :] = ed_ref[pl.ds(r, n_shift), :]
    for slot, r in enumerate(a_shifts):
        if r:
            eas_ref[slot, 0:n_shift, :] = ea_ref[pl.ds(r, n_shift), :]

    pos = pos0 + i * tm + lax.broadcasted_iota(jnp.int32, (tm, C_GROUP), 0)
    for g, w in enumerate(POOL_WINDOWS):
        cols = slice(g * C_GROUP, (g + 1) * C_GROUP)
        tok = ec_ref[new, cols]
        s = tok
        for j in range(1, w):
            s = s + ec_ref[pl.ds(HALO - j, tm), cols]
        cnt = jnp.minimum(pos + 1, w).astype(_F32)
        pp_ref[0:tm, :] = s / cnt - tok
        y = jnp.dot(pp_ref[...].astype(_BF16), pw_ref[g], preferred_element_type=_F32)[0:tm]
        yc_ref[0, :, cols] = ((y * cs_ref[:, cols]) * _silu(zc_ref[0, :, cols])).astype(_BF16)

    rc = min(tm, 32)

    def conv_rows(w_ref, e_ref, slot_of, offs, r0):
        acc = None
        for k, off in enumerate(offs):
            term = w_ref[k:k + 1, :] * e_ref[slot_of(off % 8), pl.ds(off // 8 * 8 + r0, rc), :]
            acc = term if acc is None else acc + term
        return acc

    def chunk(r0):
        rows = pl.ds(r0, rc)
        conv = conv_rows(aw_ref, eas_ref, a_shifts.index, a_offs, r0)
        ya_ref[0, rows, :] = (ba_ref[0, rows, :] * conv * _silu(za_ref[0, rows, :])).astype(_BF16)
        d_offs = [HALO - (D_CONV - 1) + k for k in range(D_CONV)]
        x = conv_rows(dw_ref, eds_ref, lambda r: r, d_offs, r0) + db_ref[...]
        mu = jnp.mean(x, axis=-1, keepdims=True)
        xc = x - mu
        var = jnp.mean(xc * xc, axis=-1, keepdims=True)
        y = xc * lax.rsqrt(var + EPS) * lg_ref[...] + lb_ref[...]
        yd_ref[0, rows, :] = (_silu(y) * _silu(zd_ref[0, rows, :])).astype(_BF16)

    if tm == rc:
        chunk(0)
    else:
        def body(c, carry):
            chunk(pl.multiple_of(c * rc, rc))
            return carry
        lax.fori_loop(0, tm // rc, body, 0)

    @pl.when(i == pl.num_programs(1) - 1)
    def _():
        na_ref[0] = ea_ref[pl.ds(HALO + tm - (A_CONV - 1), A_CONV - 1), :]
        nc_ref[0] = ec_ref[pl.ds(HALO + tm - POOL_PAST, POOL_PAST), :]
        nd_ref[0] = ed_ref[pl.ds(HALO + tm - (D_CONV - 1), D_CONV - 1), :]


def _branches(h3, states, wl, pos0, tm):
    b, t_len, _ = h3.shape

    def col(cb):
        return pl.BlockSpec((1, tm, BR_W), lambda i, j: (i, j, cb))

    def full(shape):
        return pl.BlockSpec(shape, lambda i, j: (0,) * len(shape))

    state = pl.BlockSpec((1, HALO, BR_W), lambda i, j: (i, 0, 0))
    y_out = pl.BlockSpec((1, tm, BR_W), lambda i, j: (i, j, 0))

    def st_out(n):
        return pl.BlockSpec((1, n, BR_W), lambda i, j: (i, 0, 0))

    tmc = max(tm, 8)
    n_pool = len(POOL_WINDOWS)
    return pl.pallas_call(
        functools.partial(_branch_kernel, tm=tm, pos0=pos0),
        grid=(b, t_len // tm),
        in_specs=[col(COL_VA), col(COL_CA), col(COL_BA), col(COL_ZA), col(COL_UC), col(COL_ZC),
                  col(COL_GA), col(COL_GB), col(COL_ZD), state, state, state,
                  full((A_CONV, BR_W)), full((n_pool, C_GROUP, C_GROUP)), full((1, BR_W)),
                  full((D_CONV, BR_W)), full((1, BR_W)), full((1, BR_W)), full((1, BR_W))],
        out_specs=[y_out, y_out, y_out, st_out(A_CONV - 1), st_out(POOL_PAST), st_out(D_CONV - 1)],
        out_shape=[jax.ShapeDtypeStruct((b, t_len, BR_W), _BF16)] * 3
        + [jax.ShapeDtypeStruct((b, n, BR_W), _F32) for n in (A_CONV - 1, POOL_PAST, D_CONV - 1)],
        scratch_shapes=[pltpu.VMEM((A_CONV, HALO + tmc, BR_W), _F32), pltpu.VMEM((HALO + tmc, BR_W), _F32),
                        pltpu.VMEM((8, HALO + tmc, BR_W), _F32), pltpu.VMEM((tmc, C_GROUP), _F32)],
        compiler_params=_params(("arbitrary", "arbitrary")),
        name="branches",
    )(*([h3] * 9), *states, wl["a_conv_w"], wl["c_pool_w"], wl["c_scale"], wl["d_conv_w"],
      wl["d_conv_b"], wl["d_ln_g"], wl["d_ln_b"])


def _merge_kernel(ya_ref, yb_ref, yc_ref, yd_ref, wa_ref, wb_ref, wc_ref, wd_ref,
                  g0_ref, g1_ref, g2_ref, g3_ref, o_ref):
    acc = None
    for y_ref, w_ref, g_ref in ((ya_ref, wa_ref, g0_ref), (yb_ref, wb_ref, g1_ref),
                                (yc_ref, wc_ref, g2_ref), (yd_ref, wd_ref, g3_ref)):
        term = _sigmoid(g_ref[...]) * jnp.dot(y_ref[...].astype(_BF16), w_ref[...],
                                                    preferred_element_type=_F32)
        acc = term if acc is None else acc + term
    o_ref[...] = acc.astype(_BF16)


def _merge(ys, w_brs, h2, layer, tm, tn=512):
    m = h2.shape[0]
    gate0 = COL_GATE * BR_W // tn
    y_spec = pl.BlockSpec((tm, BR_W), lambda i, j: (i, 0))
    w_spec = pl.BlockSpec((None, BR_W, tn), lambda i, j: (layer, 0, j))

    def gate(n):
        return pl.BlockSpec((tm, tn), lambda i, j: (i, gate0 + n * (D_MODEL // tn) + j))

    return pl.pallas_call(
        _merge_kernel,
        grid=(m // tm, D_MODEL // tn),
        in_specs=[y_spec] * 4 + [w_spec] * 4 + [gate(n) for n in range(N_BRANCH)],
        out_specs=pl.BlockSpec((tm, tn), lambda i, j: (i, j)),
        out_shape=jax.ShapeDtypeStruct((m, D_MODEL), _BF16),
        compiler_params=_params(("parallel", "arbitrary")),
        name="merge",
    )(*ys, *w_brs, h2, h2, h2, h2)


def _outproj_kernel(m_ref, w_ref, x_ref, o_ref):
    o_ref[...] = x_ref[...] + jnp.dot(m_ref[...], w_ref[...], preferred_element_type=_F32)


def _outproj(merged, w_out, x2, layer, tm, tn=1024):
    m = x2.shape[0]
    return pl.pallas_call(
        _outproj_kernel,
        grid=(m // tm, D_MODEL // tn),
        in_specs=[pl.BlockSpec((tm, D_MODEL), lambda i, j: (i, 0)),
                  pl.BlockSpec((None, D_MODEL, tn), lambda i, j: (layer, 0, j)),
                  pl.BlockSpec((tm, tn), lambda i, j: (i, j))],
        out_specs=pl.BlockSpec((tm, tn), lambda i, j: (i, j)),
        out_shape=jax.ShapeDtypeStruct((m, D_MODEL), _F32),
        compiler_params=_params(("parallel", "arbitrary")),
        name="outproj",
    )(merged, w_out, x2)


def _layer(x3, h2, layer, past, wl, big, kv_bufs):
    b, t_len, _ = x3.shape
    m = b * t_len
    x2 = x3.reshape(m, D_MODEL)
    tm_mat = min(m, 1024)
    h3 = h2.reshape(b, t_len, N_IN)

    if past is None:
        yb, k_buf, v_buf = _attn_prompt(h3, wl["q_norm_g"], wl["k_norm_g"], wl["inv"], layer, kv_bufs)
        states = [jnp.zeros((b, HALO, BR_W), _F32)] * 3
        pos0, tm_br = 0, 256
    else:
        h4 = h2.reshape(b, t_len, N_IN // LANES, LANES)
        yb, k_buf, v_buf = _attn_sample(h4, wl["q_norm_g"], wl["k_norm_g"], wl["inv"], past[0], past[1],
                                        layer, kv_bufs)
        states = [jnp.pad(s[layer], ((0, 0), (HALO - s.shape[2], 0), (0, 0))) for s in past[2:]]
        pos0, tm_br = PAST_LEN, t_len
    yb = yb.reshape(m, BR_W)

    ya, yc, yd, new_a, new_c, new_d = _branches(h3, states, wl, pos0, tm_br)
    ys = (ya.reshape(m, BR_W), yb, yc.reshape(m, BR_W), yd.reshape(m, BR_W))
    merged = _merge(ys, big["w_br"], h2, layer, tm_mat)
    out = _outproj(merged, big["w_out"], x2, layer, tm_mat)
    return out.reshape(b, t_len, D_MODEL), (k_buf, v_buf), (new_a, new_c, new_d)


def kernel(x_prompt, x_sample, cache_attn_k, cache_attn_v, state_conv_a, state_pool_c, state_conv_d,
           norm_g, w_in, q_norm_g, k_norm_g, a_conv_w, c_pool_w, c_scale, d_conv_w, d_conv_b,
           d_ln_g, d_ln_b, w_br_a, w_br_b, w_br_c, w_br_d, w_out):
    half = ROT_DIM // 2
    inv = ROPE_THETA ** (-(jnp.arange(half, dtype=_F32) / half))
    inv = jnp.concatenate([inv, inv, jnp.zeros((LANES - ROT_DIM,), _F32)])[None, :]
    big = {"w_out": w_out.astype(_BF16), "w_br": tuple(w.astype(_BF16) for w in (w_br_a, w_br_b, w_br_c, w_br_d))}
    pool_w = c_pool_w.astype(_BF16)
    past = (cache_attn_k, cache_attn_v, state_conv_a, state_pool_c, state_conv_d)

    hp, hs = x_prompt, x_sample
    kv_p = None
    kv_s = _cache_shift(cache_attn_k, cache_attn_v, x_sample.shape[1])
    st_p = [[] for _ in range(3)]
    st_s = [[] for _ in range(3)]
    for l in range(DEPTH):
        wl = {"norm_g": norm_g[l][None], "q_norm_g": q_norm_g[l][None], "k_norm_g": k_norm_g[l][None],
              "inv": inv, "a_conv_w": a_conv_w[l], "c_pool_w": pool_w[l], "c_scale": c_scale[l][None],
              "d_conv_w": d_conv_w[l], "d_conv_b": d_conv_b[l][None], "d_ln_g": d_ln_g[l][None],
              "d_ln_b": d_ln_b[l][None]}
        h_s, w_in_bf16 = _inproj_cast(hs.reshape(-1, D_MODEL), wl["norm_g"], w_in, l)
        h_p = _inproj(hp.reshape(-1, D_MODEL), wl["norm_g"], w_in_bf16, 1024)
        hp, kv_p, sp = _layer(hp, h_p, l, None, wl, big, kv_p)
        hs, kv_s, ss = _layer(hs, h_s, l, past, wl, big, kv_s)
        for n in range(3):
            st_p[n].append(sp[n])
            st_s[n].append(ss[n])
    b, seq = x_prompt.shape[:2]
    kv_p = [a.reshape(DEPTH, b, seq, N_HEADS, HEAD_DIM) for a in kv_p]
    kv_s = [a.reshape(cache_attn_k.shape) for a in kv_s]
    return (hp, hs, *kv_p, *(jnp.stack(s, axis=0) for s in st_p),
            *kv_s, *(jnp.stack(s, axis=0) for s in st_s))
```

```python
import functools

import jax
import jax.numpy as jnp
from jax import lax
from jax.experimental import pallas as pl
from jax.experimental.pallas import tpu as pltpu

D_MODEL = 2048
DEPTH = 2
PAST_LEN = 16384
BR_W = D_MODEL // 2
N_BRANCH = 4
A_CONV = 3
HEAD_DIM = 128
N_HEADS = BR_W // HEAD_DIM
GROUPS = ((128, 1), (512, 4), (2048, 16))
N_GROUPS = len(GROUPS)
ROT_DIM = HEAD_DIM // 4
ROPE_THETA = 500000.0
POOL_WINDOWS = (2, 4, 8, 16)
C_GROUP = BR_W // len(POOL_WINDOWS)
POOL_PAST = POOL_WINDOWS[-1] - 1
D_CONV = 31
EPS = 1e-6
N_IN = 4 * BR_W + (N_GROUPS + 3) * BR_W + 2 * BR_W + 3 * BR_W + N_BRANCH * D_MODEL

COL_VA, COL_CA, COL_BA, COL_ZA = 0, 1, 2, 3
COL_Q, COL_K, COL_V, COL_ZB = 4, 7, 8, 9
COL_UC, COL_ZC = 10, 11
COL_GA, COL_GB, COL_ZD = 12, 13, 14
COL_GATE = 15

LANES = 128
QB = 128
SAMPLE_RECENT = 512
HALO = 32
NEG = -1e30
VMEM_LIMIT = 56 * 1024 * 1024

_BF16 = jnp.bfloat16
_F32 = jnp.float32


def _sigmoid(x):
    return 0.5 * jnp.tanh(0.5 * x) + 0.5


def _silu(x):
    return x * _sigmoid(x)


def _params(sem):
    return pltpu.CompilerParams(dimension_semantics=sem, vmem_limit_bytes=VMEM_LIMIT)


def _shift_block(src, nxt, dst, last, t_len):
    nj = src.shape[2]
    keep = 16 - t_len
    dst[0, 0, :, 0:keep] = src[0, 0, :, t_len:16]
    dst[0, 0, 0:nj - 1, keep:16] = src[0, 0, 1:nj, 0:t_len]
    dst[0, 0, nj - 1, keep:16] = jnp.where(last, 0.0, nxt[0, 0, 0, 0:t_len])


def _inproj_kernel(x_ref, g_ref, w_ref, *rest, shift):
    if shift is None:
        h_ref, xn_ref = rest
    else:
        k_ref, kn_ref, v_ref, vn_ref, h_ref, ok_ref, ov_ref, xn_ref = rest
    j = pl.program_id(1)

    @pl.when(j == 0)
    def _():
        x = x_ref[...]
        y = x * lax.rsqrt(jnp.mean(x * x, axis=-1, keepdims=True) + EPS)
        xn_ref[...] = (y * g_ref[...]).astype(_BF16)

    if shift is not None:
        t_len, per_layer, n_steps = shift
        last = lax.rem(jnp.minimum(j, n_steps - 1), per_layer) == per_layer - 1
        _shift_block(k_ref, kn_ref, ok_ref, last, t_len)
        _shift_block(v_ref, vn_ref, ov_ref, last, t_len)

    h_ref[...] = jnp.dot(xn_ref[...], w_ref[...], preferred_element_type=_F32)


def _inproj(x2, g, w_bf16, tm, caches=None, t_len=None, tn=1024, nj_blk=16):
    m = x2.shape[0]
    n_i, n_j = m // tm, N_IN // tn
    in_specs = [
        pl.BlockSpec((tm, D_MODEL), lambda i, j: (i, 0)),
        pl.BlockSpec((1, D_MODEL), lambda i, j: (0, 0)),
        pl.BlockSpec((D_MODEL, tn), lambda i, j: (0, j)),
    ]
    out_specs = [pl.BlockSpec((tm, tn), lambda i, j: (i, j))]
    out_shape = [jax.ShapeDtypeStruct((m, N_IN), _F32)]
    operands = [x2, g, w_bf16]
    shift = None
    if caches is not None:
        depth, b, n_rows = caches[0].shape[:3]
        nj = n_rows // 16
        per_layer = nj // nj_blk
        n_steps = depth * per_layer
        assert b == n_i and n_steps <= n_j and t_len < 16 and nj % nj_blk == 0
        view = (depth, b, nj, 16, N_HEADS, LANES)

        def where(j):
            s = jnp.minimum(j, n_steps - 1)
            return s // per_layer, s % per_layer

        main = pl.BlockSpec((1, 1, nj_blk, 16, N_HEADS, LANES),
                            lambda i, j: (where(j)[0], i, where(j)[1], 0, 0, 0))
        nxt = pl.BlockSpec((1, 1, 1, 16, N_HEADS, LANES),
                           lambda i, j: (where(j)[0], i, jnp.minimum((where(j)[1] + 1) * nj_blk, nj - 1), 0, 0, 0))
        in_specs += [main, nxt, main, nxt]
        out_specs += [main, main]
        out_shape += [jax.ShapeDtypeStruct(view, _F32)] * 2
        kc, vc = caches[0].reshape(view), caches[1].reshape(view)
        operands += [kc, kc, vc, vc]
        shift = (t_len, per_layer, n_steps)
    outs = pl.pallas_call(
        functools.partial(_inproj_kernel, shift=shift),
        grid=(n_i, n_j),
        in_specs=in_specs,
        out_specs=out_specs,
        out_shape=out_shape,
        scratch_shapes=[pltpu.VMEM((tm, D_MODEL), _BF16)],
        compiler_params=_params(("parallel", "arbitrary")),
        name="inproj",
    )(*operands)
    return outs[0] if caches is None else outs


def _inproj_cast_kernel(x_ref, g_ref, w_ref, h_ref, wb_ref, xn_ref):
    @pl.when(pl.program_id(0) == 0)
    def _():
        x = x_ref[...]
        y = x * lax.rsqrt(jnp.mean(x * x, axis=-1, keepdims=True) + EPS)
        xn_ref[...] = (y * g_ref[...]).astype(_BF16)

    wb = w_ref[...].astype(_BF16)
    wb_ref[...] = wb
    h_ref[...] = jnp.dot(xn_ref[...], wb, preferred_element_type=_F32)


def _inproj_cast(x2, g, w_in, layer, tn=1024):
    m = x2.shape[0]
    return pl.pallas_call(
        _inproj_cast_kernel,
        grid=(N_IN // tn,),
        in_specs=[
            pl.BlockSpec((m, D_MODEL), lambda j: (0, 0)),
            pl.BlockSpec((1, D_MODEL), lambda j: (0, 0)),
            pl.BlockSpec((None, D_MODEL, tn), lambda j: (layer, 0, j)),
        ],
        out_specs=[pl.BlockSpec((m, tn), lambda j: (0, j)), pl.BlockSpec((D_MODEL, tn), lambda j: (0, j))],
        out_shape=[jax.ShapeDtypeStruct((m, N_IN), _F32), jax.ShapeDtypeStruct((D_MODEL, N_IN), _BF16)],
        scratch_shapes=[pltpu.VMEM((m, D_MODEL), _BF16)],
        compiler_params=_params(("arbitrary",)),
        name="inproj_cast",
    )(x2, g, w_in)


def _rope_tables(pos, lane, inv):
    ang = pos * inv
    cos_t = jnp.where(lane < ROT_DIM, jnp.cos(ang), 1.0)
    sin_t = jnp.where(lane < ROT_DIM, jnp.sin(ang), 0.0)
    return cos_t, sin_t


def _rotate_half_matrix():
    half = ROT_DIM // 2
    m = lax.broadcasted_iota(jnp.int32, (LANES, LANES), 0)
    l = lax.broadcasted_iota(jnp.int32, (LANES, LANES), 1)
    return jnp.where(m == l + half, jnp.where(l < half, -1.0, 0.0),
                     jnp.where(m == l - half, jnp.where(l < ROT_DIM, 1.0, 0.0), 0.0))


def _rotate_half_lanes(x):
    ax = x.ndim - 1
    half = ROT_DIM // 2
    lane = lax.broadcasted_iota(jnp.int32, x.shape, ax)
    return jnp.where(lane < half, -pltpu.roll(x, LANES - half, ax), pltpu.roll(x, half, ax))


def _dot_hi_lo(x, w2):
    hi = x.astype(_BF16)
    lo = (x - hi.astype(_F32)).astype(_BF16)
    return jnp.dot(jnp.concatenate([hi, lo], axis=1), w2, preferred_element_type=_F32)


def _rms_heads(x, g):
    return x * lax.rsqrt(jnp.mean(x * x, axis=-1, keepdims=True) + EPS) * g


def _attn_block(q, k, v, bias):
    s = lax.dot_general(q.astype(_BF16), k.astype(_BF16), (((1,), (1,)), ((), ())),
                        preferred_element_type=_F32) + bias
    m = jnp.max(s, axis=-1, keepdims=True)
    p = jnp.exp(s - m).astype(_BF16)
    v1 = jnp.concatenate([v.astype(_BF16), jnp.ones(v.shape, _BF16)], axis=1)
    acc = jnp.dot(p, v1, preferred_element_type=_F32)
    l = acc[:, LANES:]
    return acc[:, :LANES] * (1.0 / l), m + jnp.log(l)


def _attn_prompt_kernel(*refs, seq, n_alias):
    q0_ref, q1_ref, q2_ref, k_ref, v_ref, zb_ref, qg_ref, kg_ref, inv_ref = refs[:9]
    yb_ref, ko_ref, vo_ref = refs[9 + n_alias:12 + n_alias]
    cos_ref, sin_ref, band_ref, causal_ref, qs_ref, og_ref, lg_ref = refs[12 + n_alias:]
    rows = 256
    n_chunks = seq // rows

    @pl.when((pl.program_id(0) == 0) & (pl.program_id(1) == 0))
    def _():
        def body(c, carry):
            r0 = pl.multiple_of(c * rows, rows)
            pos = (lax.broadcasted_iota(jnp.int32, (rows, LANES), 0) + r0).astype(_F32)
            lane = lax.broadcasted_iota(jnp.int32, (rows, LANES), 1)
            cos_t, sin_t = _rope_tables(pos, lane, inv_ref[...])
            cos_ref[pl.ds(r0, rows), :] = cos_t
            sin_ref[pl.ds(r0, rows), :] = sin_t
            return carry
        lax.fori_loop(0, n_chunks, body, 0)
        ri = lax.broadcasted_iota(jnp.int32, (QB, 2 * QB), 0)
        ci = lax.broadcasted_iota(jnp.int32, (QB, 2 * QB), 1)
        band_ref[...] = jnp.where(ci < ri, NEG, jnp.where(ci > ri + QB, NEG, 0.0))
        rq = lax.broadcasted_iota(jnp.int32, (QB, QB), 0)
        cq = lax.broadcasted_iota(jnp.int32, (QB, QB), 1)
        causal_ref[...] = jnp.where(cq > rq, NEG, 0.0)

    scale = HEAD_DIM ** -0.5
    rot_m = _rotate_half_matrix().astype(_BF16)
    rot2 = jnp.concatenate([rot_m, rot_m], axis=0)
    ones2 = jnp.ones((2 * LANES, LANES), _BF16)

    def norm_rope(x, g, cos_t, sin_t):
        ssq = _dot_hi_lo(x * x, ones2)
        y = x * lax.rsqrt(ssq * (1.0 / HEAD_DIM) + EPS) * g
        return y * cos_t + _dot_hi_lo(y, rot2) * sin_t

    def prep(c, carry):
        r0 = pl.multiple_of(c * rows, rows)
        sl = pl.ds(r0, rows)
        cos_t, sin_t = cos_ref[sl, :], sin_ref[sl, :]
        for g, q_ref in enumerate((q0_ref, q1_ref, q2_ref)):
            qs_ref[g, sl, :] = norm_rope(q_ref[0, sl, :], qg_ref[...], cos_t, sin_t) * scale
        ko_ref[0, 0, sl, :] = norm_rope(k_ref[0, sl, :], kg_ref[...], cos_t, sin_t)
        vo_ref[0, 0, sl, :] = v_ref[0, sl, :]
        for l2 in range(1, ko_ref.shape[0]):
            ko_ref[l2, 0, sl, :] = jnp.zeros((rows, LANES), _F32)
            vo_ref[l2, 0, sl, :] = jnp.zeros((rows, LANES), _F32)
        return carry
    lax.fori_loop(0, n_chunks, prep, 0, unroll=True)

    def rows_of(start, n, d):
        return pl.ds(start, n) if d == 1 else pl.ds(start, n, stride=d)

    for g, (w, d) in enumerate(GROUPS):
        assert w // d == QB
        n_blk = seq // (d * QB)
        for r in range(d):
            for blk in range(n_blk):
                q = qs_ref[g, rows_of(r + d * QB * blk, QB, d), :]
                if blk == 0:
                    ksl, bias = rows_of(r, QB, d), causal_ref[...]
                else:
                    ksl, bias = rows_of(r + d * QB * (blk - 1), 2 * QB, d), band_ref[...]
                o, lse = _attn_block(q, ko_ref[0, 0, ksl, :], v_ref[0, ksl, :], bias)
                osl = rows_of(r + d * QB * blk, QB, d)
                og_ref[g, osl, :] = o
                lg_ref[g, osl, :] = lse

    def merge(c, carry):
        r0 = pl.multiple_of(c * rows, rows)
        sl = pl.ds(r0, rows)
        l0, l1, l2 = lg_ref[0, sl, :], lg_ref[1, sl, :], lg_ref[2, sl, :]
        mx = jnp.maximum(jnp.maximum(l0, l1), l2)
        w0, w1, w2 = jnp.exp(l0 - mx), jnp.exp(l1 - mx), jnp.exp(l2 - mx)
        o = (w0 * og_ref[0, sl, :] + w1 * og_ref[1, sl, :] + w2 * og_ref[2, sl, :]) / (w0 + w1 + w2)
        yb_ref[0, sl, :] = (o * _silu(zb_ref[0, sl, :])).astype(_BF16)
        return carry
    lax.fori_loop(0, n_chunks, merge, 0)


def _attn_prompt(h3, qg, kg, inv, layer, kv_bufs):
    b, seq, _ = h3.shape
    hb = BR_W // LANES
    first = kv_bufs is None
    assert first == (layer == 0)

    def col(cb):
        return pl.BlockSpec((1, seq, LANES), lambda i, j: (i, 0, cb * hb + j))

    small = pl.BlockSpec((1, LANES), lambda i, j: (0, 0))
    any_spec = pl.BlockSpec(memory_space=pl.ANY)
    kv_out = pl.BlockSpec((DEPTH if first else 1, 1, seq, LANES), lambda i, j: (layer, i, 0, j))
    kv_shape = jax.ShapeDtypeStruct((DEPTH, b, seq, BR_W), _F32)
    n_alias = 0 if first else 2
    return pl.pallas_call(
        functools.partial(_attn_prompt_kernel, seq=seq, n_alias=n_alias),
        grid=(b, N_HEADS),
        in_specs=[col(COL_Q), col(COL_Q + 1), col(COL_Q + 2), col(COL_K), col(COL_V), col(COL_ZB),
                  small, small, small] + [any_spec] * n_alias,
        out_specs=[pl.BlockSpec((1, seq, LANES), lambda i, j: (i, 0, j)), kv_out, kv_out],
        out_shape=[jax.ShapeDtypeStruct((b, seq, BR_W), _BF16), kv_shape, kv_shape],
        input_output_aliases={} if first else {9: 1, 10: 2},
        scratch_shapes=[pltpu.VMEM((seq, LANES), _F32)] * 2
        + [pltpu.VMEM((QB, 2 * QB), _F32), pltpu.VMEM((QB, QB), _F32)]
        + [pltpu.VMEM((N_GROUPS, seq, LANES), _F32)] * 3,
        compiler_params=_params(("arbitrary", "arbitrary")),
        name="attn_prompt",
    )(h3, h3, h3, h3, h3, h3, qg, kg, inv, *(() if first else kv_bufs))


def _attn_sample_kernel(q0_ref, q1_ref, q2_ref, k_ref, v_ref, zb_ref, qg_ref, kg_ref, inv_ref,
                        kres_ref, krec_ref, vres_ref, vrec_ref, kbuf_hbm, vbuf_hbm,
                        yb_ref, ok_hbm, ov_hbm, kn_ref, vn_ref, sem, *, t_len, n_rows, layer):
    b = pl.program_id(0)
    shape = (t_len, N_HEADS, LANES)
    pos = (lax.broadcasted_iota(jnp.int32, shape, 0) + PAST_LEN).astype(_F32)
    lane = lax.broadcasted_iota(jnp.int32, shape, 2)
    cos_t, sin_t = _rope_tables(pos, lane, inv_ref[...].reshape(1, 1, LANES))

    def norm_rope(x, g_ref):
        y = _rms_heads(x, g_ref[...].reshape(1, 1, LANES))
        return y * cos_t + _rotate_half_lanes(y) * sin_t

    scale = HEAD_DIM ** -0.5
    k_new = norm_rope(k_ref[0], kg_ref)
    v_new = v_ref[0]
    kn_ref[...] = k_new
    vn_ref[...] = v_new
    nj = n_rows // 16
    tail = pl.ds(16 - t_len, t_len)
    copies = [pltpu.make_async_copy(kn_ref, ok_hbm.at[layer, b, nj - 1, tail], sem.at[0]),
              pltpu.make_async_copy(vn_ref, ov_hbm.at[layer, b, nj - 1, tail], sem.at[1])]
    for cp in copies:
        cp.start()

    qs = [norm_rope(q_ref[0], qg_ref) * scale for q_ref in (q0_ref, q1_ref, q2_ref)]

    for t in range(t_len):
        outs, lses = [], []
        for g, (w, d) in enumerate(GROUPS):
            q = qs[g][t]
            pieces = []
            nrec = krec_ref.shape[2]
            if d == 1:
                lo = nrec - QB // 16
                kc = krec_ref[0, 0, lo:nrec].reshape(QB, N_HEADS, LANES)
                vc = vrec_ref[0, 0, lo:nrec].reshape(QB, N_HEADS, LANES)
                idx = lax.broadcasted_iota(jnp.int32, (QB, N_HEADS, 1), 0)
                pieces.append((kc, vc, idx >= t))
                pieces.append((k_new[:t + 1], v_new[:t + 1], None))
            elif d < 16:
                per = 16 // d
                lo = nrec - (w // d) // per
                for m in range(per):
                    pieces.append((krec_ref[0, 0, lo:nrec, t + d * m], vrec_ref[0, 0, lo:nrec, t + d * m], None))
                pieces.append((k_new[t:t + 1], v_new[t:t + 1], None))
            else:
                assert d == 16
                lo = nj - w // 16
                pieces.append((kres_ref[0, 0, lo:nj, t], vres_ref[0, 0, lo:nj, t], None))
                pieces.append((k_new[t:t + 1], v_new[t:t + 1], None))
            scores = []
            for kk, vv, mask in pieces:
                s = jnp.sum(kk * q[None], axis=-1, keepdims=True)
                if mask is not None:
                    s = jnp.where(mask, s, NEG)
                scores.append(s)
            mx = functools.reduce(jnp.maximum, [jnp.max(s, axis=0) for s in scores])
            den = 0.0
            acc = 0.0
            for s, (kk, vv, mask) in zip(scores, pieces):
                p = jnp.exp(s - mx[None])
                den = den + jnp.sum(p, axis=0)
                acc = acc + jnp.sum(p * vv, axis=0)
            outs.append(acc / den)
            lses.append(mx + jnp.log(den))
        mx = jnp.maximum(jnp.maximum(lses[0], lses[1]), lses[2])
        ws = [jnp.exp(l - mx) for l in lses]
        o = (ws[0] * outs[0] + ws[1] * outs[1] + ws[2] * outs[2]) / (ws[0] + ws[1] + ws[2])
        yb_ref[0, t] = o * _silu(zb_ref[0, t])

    for cp in copies:
        cp.wait()


def _attn_sample(h4, qg, kg, inv, cache_k, cache_v, layer, kv_bufs):
    b, t_len = h4.shape[:2]
    n_rows = cache_k.shape[2]
    kc = cache_k.reshape(DEPTH, b, n_rows // 16, 16, N_HEADS, LANES)
    vc = cache_v.reshape(DEPTH, b, n_rows // 16, 16, N_HEADS, LANES)

    def col(cb):
        return pl.BlockSpec((1, t_len, N_HEADS, LANES), lambda i: (i, 0, cb, 0))

    small = pl.BlockSpec((1, LANES), lambda i: (0, 0))
    nj = n_rows // 16
    nrec = SAMPLE_RECENT // 16
    assert nj % nrec == 0 and 16 % t_len == 0 and all(w <= SAMPLE_RECENT for w, d in GROUPS if d < 16)
    resid = pl.BlockSpec((1, 1, nj, t_len, N_HEADS, LANES), lambda i: (layer, i, 0, 0, 0, 0))
    recent = pl.BlockSpec((1, 1, nrec, 16, N_HEADS, LANES), lambda i: (layer, i, nj // nrec - 1, 0, 0, 0))
    any_spec = pl.BlockSpec(memory_space=pl.ANY)
    rows_shape = (t_len, N_HEADS, LANES)
    return pl.pallas_call(
        functools.partial(_attn_sample_kernel, t_len=t_len, n_rows=n_rows, layer=layer),
        grid=(b,),
        in_specs=[col(COL_Q), col(COL_Q + 1), col(COL_Q + 2), col(COL_K), col(COL_V), col(COL_ZB),
                  small, small, small, resid, recent, resid, recent, any_spec, any_spec],
        out_specs=[pl.BlockSpec((1, t_len, N_HEADS, LANES), lambda i: (i, 0, 0, 0)), any_spec, any_spec],
        out_shape=[jax.ShapeDtypeStruct((b, t_len, N_HEADS, LANES), _F32),
                   jax.ShapeDtypeStruct(kc.shape, _F32), jax.ShapeDtypeStruct(vc.shape, _F32)],
        input_output_aliases={13: 1, 14: 2},
        scratch_shapes=[pltpu.VMEM(rows_shape, _F32)] * 2 + [pltpu.SemaphoreType.DMA((2,))],
        compiler_params=_params(("arbitrary",)),
        name="attn_sample",
    )(h4, h4, h4, h4, h4, h4, qg, kg, inv, kc, kc, vc, vc, *kv_bufs)


def _branch_kernel(va_ref, ca_ref, ba_ref, za_ref, uc_ref, zc_ref, ga_ref, gb_ref, zd_ref,
                   sta_ref, stc_ref, std_ref, aw_ref, pw_ref, cs_ref, dw_ref, db_ref, lg_ref, lb_ref,
                   ya_ref, yc_ref, yd_ref, na_ref, nc_ref, nd_ref,
                   eas_ref, ec_ref, eds_ref, pp_ref, *, tm, pos0):
    i = pl.program_id(1)
    ea_ref = eas_ref.at[0]
    ed_ref = eds_ref.at[0]
    a_offs = [HALO - (A_CONV - 1) + k for k in range(A_CONV)]
    a_shifts = sorted({off % 8 for off in a_offs} | {0})
    assert len(a_shifts) <= eas_ref.shape[0]

    @pl.when(i == 0)
    def _():
        ea_ref[0:HALO, :] = sta_ref[0]
        ec_ref[0:HALO, :] = stc_ref[0]
        ed_ref[0:HALO, :] = std_ref[0]
        pp_ref[...] = jnp.zeros_like(pp_ref)

    @pl.when(i > 0)
    def _():
        for e_ref in (ea_ref, ec_ref, ed_ref):
            e_ref[0:HALO, :] = e_ref[tm:tm + HALO, :]

    new = pl.ds(HALO, tm)
    ea_ref[new, :] = ca_ref[0] * va_ref[0]
    ec_ref[new, :] = uc_ref[0]
    ed_ref[new, :] = ga_ref[0] * _sigmoid(gb_ref[0])
    n_shift = HALO + tm - 8
    for r in range(1, 8):
        eds_ref[r, 0:n_shift, :] = ed_ref[pl.ds(r, n_shift), :]
    for slot, r in enumerate(a_shifts):
        if r:
            eas_ref[slot, 0:n_shift, :] = ea_ref[pl.ds(r, n_shift), :]

    pos = pos0 + i * tm + lax.broadcasted_iota(jnp.int32, (tm, C_GROUP), 0)
    for g, w in enumerate(POOL_WINDOWS):
        cols = slice(g * C_GROUP, (g + 1) * C_GROUP)
        tok = ec_ref[new, cols]
        s = tok
        for j in range(1, w):
            s = s + ec_ref[pl.ds(HALO - j, tm), cols]
        cnt = jnp.minimum(pos + 1, w).astype(_F32)
        pp_ref[0:tm, :] = s / cnt - tok
        y = jnp.dot(pp_ref[...].astype(_BF16), pw_ref[g], preferred_element_type=_F32)[0:tm]
        yc_ref[0, :, cols] = ((y * cs_ref[:, cols]) * _silu(zc_ref[0, :, cols])).astype(_BF16)

    rc = min(tm, 32)

    def conv_rows(w_ref, e_ref, slot_of, offs, r0):
        sub = min(rc, 8)
        accs = [None] * (rc // sub)
        for k, off in enumerate(offs):
            w = w_ref[k, 0:sub, :]
            for n in range(rc // sub):
                term = w * e_ref[slot_of(off % 8), pl.ds(off // 8 * 8 + r0 + sub * n, sub), :]
                accs[n] = term if accs[n] is None else accs[n] + term
        return accs[0] if len(accs) == 1 else jnp.concatenate(accs, axis=0)

    def chunk(r0):
        rows = pl.ds(r0, rc)
        conv = conv_rows(aw_ref, eas_ref, a_shifts.index, a_offs, r0)
        ya_ref[0, rows, :] = (ba_ref[0, rows, :] * conv * _silu(za_ref[0, rows, :])).astype(_BF16)
        d_offs = [HALO - (D_CONV - 1) + k for k in range(D_CONV)]
        x = conv_rows(dw_ref, eds_ref, lambda r: r, d_offs, r0) + db_ref[...]
        mu = jnp.mean(x, axis=-1, keepdims=True)
        xc = x - mu
        var = jnp.mean(xc * xc, axis=-1, keepdims=True)
        y = xc * lax.rsqrt(var + EPS) * lg_ref[...] + lb_ref[...]
        yd_ref[0, rows, :] = (_silu(y) * _silu(zd_ref[0, rows, :])).astype(_BF16)

    if tm == rc:
        chunk(0)
    else:
        def body(c, carry):
            chunk(pl.multiple_of(c * rc, rc))
            return carry
        lax.fori_loop(0, tm // rc, body, 0)

    @pl.when(i == pl.num_programs(1) - 1)
    def _():
        na_ref[0] = ea_ref[pl.ds(HALO + tm - (A_CONV - 1), A_CONV - 1), :]
        nc_ref[0] = ec_ref[pl.ds(HALO + tm - POOL_PAST, POOL_PAST), :]
        nd_ref[0] = ed_ref[pl.ds(HALO + tm - (D_CONV - 1), D_CONV - 1), :]


def _branches(h3, states, wl, pos0, tm):
    b, t_len, _ = h3.shape

    def col(cb):
        return pl.BlockSpec((1, tm, BR_W), lambda i, j: (i, j, cb))

    def full(shape):
        return pl.BlockSpec(shape, lambda i, j: (0,) * len(shape))

    state = pl.BlockSpec((1, HALO, BR_W), lambda i, j: (i, 0, 0))
    y_out = pl.BlockSpec((1, tm, BR_W), lambda i, j: (i, j, 0))

    def st_out(n):
        return pl.BlockSpec((1, n, BR_W), lambda i, j: (i, 0, 0))

    tmc = max(tm, 8)
    n_pool = len(POOL_WINDOWS)
    return pl.pallas_call(
        functools.partial(_branch_kernel, tm=tm, pos0=pos0),
        grid=(b, t_len // tm),
        in_specs=[col(COL_VA), col(COL_CA), col(COL_BA), col(COL_ZA), col(COL_UC), col(COL_ZC),
                  col(COL_GA), col(COL_GB), col(COL_ZD), state, state, state,
                  full((A_CONV, 8, BR_W)), full((n_pool, C_GROUP, C_GROUP)), full((1, BR_W)),
                  full((D_CONV, 8, BR_W)), full((1, BR_W)), full((1, BR_W)), full((1, BR_W))],
        out_specs=[y_out, y_out, y_out, st_out(A_CONV - 1), st_out(POOL_PAST), st_out(D_CONV - 1)],
        out_shape=[jax.ShapeDtypeStruct((b, t_len, BR_W), _BF16)] * 3
        + [jax.ShapeDtypeStruct((b, n, BR_W), _F32) for n in (A_CONV - 1, POOL_PAST, D_CONV - 1)],
        scratch_shapes=[pltpu.VMEM((A_CONV, HALO + tmc, BR_W), _F32), pltpu.VMEM((HALO + tmc, BR_W), _F32),
                        pltpu.VMEM((8, HALO + tmc, BR_W), _F32), pltpu.VMEM((tmc, C_GROUP), _F32)],
        compiler_params=_params(("arbitrary", "arbitrary")),
        name="branches",
    )(*([h3] * 9), *states, wl["a_conv_w"], wl["c_pool_w"], wl["c_scale"], wl["d_conv_w"],
      wl["d_conv_b"], wl["d_ln_g"], wl["d_ln_b"])


def _merge_kernel(ya_ref, yb_ref, yc_ref, yd_ref, wa_ref, wb_ref, wc_ref, wd_ref,
                  g0_ref, g1_ref, g2_ref, g3_ref, o_ref):
    acc = None
    for y_ref, w_ref, g_ref in ((ya_ref, wa_ref, g0_ref), (yb_ref, wb_ref, g1_ref),
                                (yc_ref, wc_ref, g2_ref), (yd_ref, wd_ref, g3_ref)):
        term = _sigmoid(g_ref[...]) * jnp.dot(y_ref[...].astype(_BF16), w_ref[...],
                                                    preferred_element_type=_F32)
        acc = term if acc is None else acc + term
    o_ref[...] = acc.astype(_BF16)


def _merge(ys, w_brs, h2, layer, tm, tn=512):
    m = h2.shape[0]
    gate0 = COL_GATE * BR_W // tn
    y_spec = pl.BlockSpec((tm, BR_W), lambda i, j: (i, 0))
    w_spec = pl.BlockSpec((None, BR_W, tn), lambda i, j: (layer, 0, j))

    def gate(n):
        return pl.BlockSpec((tm, tn), lambda i, j: (i, gate0 + n * (D_MODEL // tn) + j))

    return pl.pallas_call(
        _merge_kernel,
        grid=(m // tm, D_MODEL // tn),
        in_specs=[y_spec] * 4 + [w_spec] * 4 + [gate(n) for n in range(N_BRANCH)],
        out_specs=pl.BlockSpec((tm, tn), lambda i, j: (i, j)),
        out_shape=jax.ShapeDtypeStruct((m, D_MODEL), _BF16),
        compiler_params=_params(("parallel", "arbitrary")),
        name="merge",
    )(*ys, *w_brs, h2, h2, h2, h2)


def _outproj_kernel(m_ref, w_ref, x_ref, o_ref):
    o_ref[...] = x_ref[...] + jnp.dot(m_ref[...], w_ref[...], preferred_element_type=_F32)


def _outproj(merged, w_out, x2, layer, tm, tn=1024):
    m = x2.shape[0]
    return pl.pallas_call(
        _outproj_kernel,
        grid=(m // tm, D_MODEL // tn),
        in_specs=[pl.BlockSpec((tm, D_MODEL), lambda i, j: (i, 0)),
                  pl.BlockSpec((None, D_MODEL, tn), lambda i, j: (layer, 0, j)),
                  pl.BlockSpec((tm, tn), lambda i, j: (i, j))],
        out_specs=pl.BlockSpec((tm, tn), lambda i, j: (i, j)),
        out_shape=jax.ShapeDtypeStruct((m, D_MODEL), _F32),
        compiler_params=_params(("parallel", "arbitrary")),
        name="outproj",
    )(merged, w_out, x2)


def _layer(x3, h2, layer, past, wl, big, kv_bufs):
    b, t_len, _ = x3.shape
    m = b * t_len
    x2 = x3.reshape(m, D_MODEL)
    tm_mat = min(m, 1024)
    h3 = h2.reshape(b, t_len, N_IN)

    if past is None:
        yb, k_buf, v_buf = _attn_prompt(h3, wl["q_norm_g"], wl["k_norm_g"], wl["inv"], layer, kv_bufs)
        states = [jnp.zeros((b, HALO, BR_W), _F32)] * 3
        pos0, tm_br = 0, 256
    else:
        h4 = h2.reshape(b, t_len, N_IN // LANES, LANES)
        yb, k_buf, v_buf = _attn_sample(h4, wl["q_norm_g"], wl["k_norm_g"], wl["inv"], past[0], past[1],
                                        layer, kv_bufs)
        states = [jnp.pad(s[layer], ((0, 0), (HALO - s.shape[2], 0), (0, 0))) for s in past[2:]]
        pos0, tm_br = PAST_LEN, t_len
    yb = yb.reshape(m, BR_W)

    ya, yc, yd, new_a, new_c, new_d = _branches(h3, states, wl, pos0, tm_br)
    ys = (ya.reshape(m, BR_W), yb, yc.reshape(m, BR_W), yd.reshape(m, BR_W))
    merged = _merge(ys, big["w_br"], h2, layer, tm_mat)
    out = _outproj(merged, big["w_out"], x2, layer, tm_mat)
    return out.reshape(b, t_len, D_MODEL), (k_buf, v_buf), (new_a, new_c, new_d)


def kernel(x_prompt, x_sample, cache_attn_k, cache_attn_v, state_conv_a, state_pool_c, state_conv_d,
           norm_g, w_in, q_norm_g, k_norm_g, a_conv_w, c_pool_w, c_scale, d_conv_w, d_conv_b,
           d_ln_g, d_ln_b, w_br_a, w_br_b, w_br_c, w_br_d, w_out):
    half = ROT_DIM // 2
    inv = ROPE_THETA ** (-(jnp.arange(half, dtype=_F32) / half))
    inv = jnp.concatenate([inv, inv, jnp.zeros((LANES - ROT_DIM,), _F32)])[None, :]
    big = {"w_out": w_out.astype(_BF16), "w_br": tuple(w.astype(_BF16) for w in (w_br_a, w_br_b, w_br_c, w_br_d))}
    pool_w = c_pool_w.astype(_BF16)
    past = (cache_attn_k, cache_attn_v, state_conv_a, state_pool_c, state_conv_d)

    def on_sublanes(w):
        return jnp.broadcast_to(w[:, None, :], (w.shape[0], 8, w.shape[1]))

    hp, hs = x_prompt, x_sample
    kv_p = kv_s = None
    st_p = [[] for _ in range(3)]
    st_s = [[] for _ in range(3)]
    for l in range(DEPTH):
        wl = {"norm_g": norm_g[l][None], "q_norm_g": q_norm_g[l][None], "k_norm_g": k_norm_g[l][None],
              "inv": inv, "a_conv_w": on_sublanes(a_conv_w[l]), "c_pool_w": pool_w[l], "c_scale": c_scale[l][None],
              "d_conv_w": on_sublanes(d_conv_w[l]), "d_conv_b": d_conv_b[l][None], "d_ln_g": d_ln_g[l][None],
              "d_ln_b": d_ln_b[l][None]}
        h_s, w_in_bf16 = _inproj_cast(hs.reshape(-1, D_MODEL), wl["norm_g"], w_in, l)
        if l == 0:
            h_p, *kv_s = _inproj(hp.reshape(-1, D_MODEL), wl["norm_g"], w_in_bf16, 1024,
                                 caches=(cache_attn_k, cache_attn_v), t_len=x_sample.shape[1])
        else:
            h_p = _inproj(hp.reshape(-1, D_MODEL), wl["norm_g"], w_in_bf16, 1024)
        hp, kv_p, sp = _layer(hp, h_p, l, None, wl, big, kv_p)
        hs, kv_s, ss = _layer(hs, h_s, l, past, wl, big, kv_s)
        for n in range(3):
            st_p[n].append(sp[n])
            st_s[n].append(ss[n])
    b, seq = x_prompt.shape[:2]
    kv_p = [a.reshape(DEPTH, b, seq, N_HEADS, HEAD_DIM) for a in kv_p]
    kv_s = [a.reshape(cache_attn_k.shape) for a in kv_s]
    return (hp, hs, *kv_p, *(jnp.stack(s, axis=0) for s in st_p),
            *kv_s, *(jnp.stack(s, axis=0) for s in st_s))
```

```python
import functools

import jax
import jax.numpy as jnp
from jax import lax
from jax.experimental import pallas as pl
from jax.experimental.pallas import tpu as pltpu

D_MODEL = 2048
DEPTH = 2
PAST_LEN = 16384
BR_W = D_MODEL // 2
N_BRANCH = 4
A_CONV = 3
HEAD_DIM = 128
N_HEADS = BR_W // HEAD_DIM
GROUPS = ((128, 1), (512, 4), (2048, 16))
N_GROUPS = len(GROUPS)
ROT_DIM = HEAD_DIM // 4
ROPE_THETA = 500000.0
POOL_WINDOWS = (2, 4, 8, 16)
C_GROUP = BR_W // len(POOL_WINDOWS)
POOL_PAST = POOL_WINDOWS[-1] - 1
D_CONV = 31
EPS = 1e-6
N_IN = 4 * BR_W + (N_GROUPS + 3) * BR_W + 2 * BR_W + 3 * BR_W + N_BRANCH * D_MODEL

COL_VA, COL_CA, COL_BA, COL_ZA = 0, 1, 2, 3
COL_Q, COL_K, COL_V, COL_ZB = 4, 7, 8, 9
COL_UC, COL_ZC = 10, 11
COL_GA, COL_GB, COL_ZD = 12, 13, 14
COL_GATE = 15

LANES = 128
QB = 128
SAMPLE_RECENT = 512
HALO = 32
NEG = -1e30
VMEM_LIMIT = 56 * 1024 * 1024

_BF16 = jnp.bfloat16
_F32 = jnp.float32


def _sigmoid(x):
    return 0.5 * jnp.tanh(0.5 * x) + 0.5


def _silu(x):
    return x * _sigmoid(x)


def _params(sem):
    return pltpu.CompilerParams(dimension_semantics=sem, vmem_limit_bytes=VMEM_LIMIT)


def _shift_block(src, nxt, dst, last, t_len):
    nj = src.shape[2]
    keep = 16 - t_len
    dst[0, 0, :, 0:keep] = src[0, 0, :, t_len:16]
    dst[0, 0, 0:nj - 1, keep:16] = src[0, 0, 1:nj, 0:t_len]
    dst[0, 0, nj - 1, keep:16] = jnp.where(last, 0.0, nxt[0, 0, 0, 0:t_len])


def _inproj_kernel(x_ref, g_ref, w_ref, *rest, shift):
    if shift is None:
        h_ref, xn_ref = rest
    else:
        k_ref, kn_ref, v_ref, vn_ref, h_ref, ok_ref, ov_ref, xn_ref = rest
    j = pl.program_id(1)

    @pl.when(j == 0)
    def _():
        x = x_ref[...]
        y = x * lax.rsqrt(jnp.mean(x * x, axis=-1, keepdims=True) + EPS)
        xn_ref[...] = (y * g_ref[...]).astype(_BF16)

    if shift is not None:
        t_len, per_layer, n_steps = shift
        last = lax.rem(jnp.minimum(j, n_steps - 1), per_layer) == per_layer - 1
        _shift_block(k_ref, kn_ref, ok_ref, last, t_len)
        _shift_block(v_ref, vn_ref, ov_ref, last, t_len)

    h_ref[...] = jnp.dot(xn_ref[...], w_ref[...], preferred_element_type=_F32)


def _inproj(x2, g, w_bf16, tm, caches=None, t_len=None, tn=1024, nj_blk=16):
    m = x2.shape[0]
    n_i, n_j = m // tm, N_IN // tn
    in_specs = [
        pl.BlockSpec((tm, D_MODEL), lambda i, j: (i, 0)),
        pl.BlockSpec((1, D_MODEL), lambda i, j: (0, 0)),
        pl.BlockSpec((D_MODEL, tn), lambda i, j: (0, j)),
    ]
    out_specs = [pl.BlockSpec((tm, tn), lambda i, j: (i, j))]
    out_shape = [jax.ShapeDtypeStruct((m, N_IN), _F32)]
    operands = [x2, g, w_bf16]
    shift = None
    if caches is not None:
        depth, b, n_rows = caches[0].shape[:3]
        nj = n_rows // 16
        per_layer = nj // nj_blk
        n_steps = depth * per_layer
        assert b == n_i and n_steps <= n_j and t_len < 16 and nj % nj_blk == 0
        view = (depth, b, nj, 16, N_HEADS, LANES)

        def where(j):
            s = jnp.minimum(j, n_steps - 1)
            return s // per_layer, s % per_layer

        main = pl.BlockSpec((1, 1, nj_blk, 16, N_HEADS, LANES),
                            lambda i, j: (where(j)[0], i, where(j)[1], 0, 0, 0))
        nxt = pl.BlockSpec((1, 1, 1, 16, N_HEADS, LANES),
                           lambda i, j: (where(j)[0], i, jnp.minimum((where(j)[1] + 1) * nj_blk, nj - 1), 0, 0, 0))
        in_specs += [main, nxt, main, nxt]
        out_specs += [main, main]
        out_shape += [jax.ShapeDtypeStruct(view, _F32)] * 2
        kc, vc = caches[0].reshape(view), caches[1].reshape(view)
        operands += [kc, kc, vc, vc]
        shift = (t_len, per_layer, n_steps)
    outs = pl.pallas_call(
        functools.partial(_inproj_kernel, shift=shift),
        grid=(n_i, n_j),
        in_specs=in_specs,
        out_specs=out_specs,
        out_shape=out_shape,
        scratch_shapes=[pltpu.VMEM((tm, D_MODEL), _BF16)],
        compiler_params=_params(("parallel", "arbitrary")),
        name="inproj",
    )(*operands)
    return outs[0] if caches is None else outs


def _inproj_cast_kernel(x_ref, g_ref, w_ref, h_ref, wb_ref, xn_ref):
    @pl.when(pl.program_id(0) == 0)
    def _():
        x = x_ref[...]
        y = x * lax.rsqrt(jnp.mean(x * x, axis=-1, keepdims=True) + EPS)
        xn_ref[...] = (y * g_ref[...]).astype(_BF16)

    wb = w_ref[...].astype(_BF16)
    wb_ref[...] = wb
    h_ref[...] = jnp.dot(xn_ref[...], wb, preferred_element_type=_F32)


def _inproj_cast(x2, g, w_in, layer, tn=1024):
    m = x2.shape[0]
    return pl.pallas_call(
        _inproj_cast_kernel,
        grid=(N_IN // tn,),
        in_specs=[
            pl.BlockSpec((m, D_MODEL), lambda j: (0, 0)),
            pl.BlockSpec((1, D_MODEL), lambda j: (0, 0)),
            pl.BlockSpec((None, D_MODEL, tn), lambda j: (layer, 0, j)),
        ],
        out_specs=[pl.BlockSpec((m, tn), lambda j: (0, j)), pl.BlockSpec((D_MODEL, tn), lambda j: (0, j))],
        out_shape=[jax.ShapeDtypeStruct((m, N_IN), _F32), jax.ShapeDtypeStruct((D_MODEL, N_IN), _BF16)],
        scratch_shapes=[pltpu.VMEM((m, D_MODEL), _BF16)],
        compiler_params=_params(("arbitrary",)),
        name="inproj_cast",
    )(x2, g, w_in)


def _rope_tables(pos, lane, inv):
    ang = pos * inv
    cos_t = jnp.where(lane < ROT_DIM, jnp.cos(ang), 1.0)
    sin_t = jnp.where(lane < ROT_DIM, jnp.sin(ang), 0.0)
    return cos_t, sin_t


def _rotate_half_matrix():
    half = ROT_DIM // 2
    m = lax.broadcasted_iota(jnp.int32, (LANES, LANES), 0)
    l = lax.broadcasted_iota(jnp.int32, (LANES, LANES), 1)
    return jnp.where(m == l + half, jnp.where(l < half, -1.0, 0.0),
                     jnp.where(m == l - half, jnp.where(l < ROT_DIM, 1.0, 0.0), 0.0))


def _rotate_half_lanes(x):
    ax = x.ndim - 1
    half = ROT_DIM // 2
    lane = lax.broadcasted_iota(jnp.int32, x.shape, ax)
    return jnp.where(lane < half, -pltpu.roll(x, LANES - half, ax), pltpu.roll(x, half, ax))


def _dot_hi_lo(x, w2):
    hi = x.astype(_BF16)
    lo = (x - hi.astype(_F32)).astype(_BF16)
    return jnp.dot(jnp.concatenate([hi, lo], axis=1), w2, preferred_element_type=_F32)


def _rms_heads(x, g):
    return x * lax.rsqrt(jnp.mean(x * x, axis=-1, keepdims=True) + EPS) * g


def _attn_block(q, k, v, bias):
    s = lax.dot_general(q.astype(_BF16), k.astype(_BF16), (((1,), (1,)), ((), ())),
                        preferred_element_type=_F32) + bias
    m = jnp.max(s, axis=-1, keepdims=True)
    p = jnp.exp(s - m).astype(_BF16)
    v1 = jnp.concatenate([v.astype(_BF16), jnp.ones(v.shape, _BF16)], axis=1)
    acc = jnp.dot(p, v1, preferred_element_type=_F32)
    l = acc[:, LANES:]
    return acc[:, :LANES] * (1.0 / l), m + jnp.log(l)


def _attn_prompt_kernel(*refs, seq, n_alias):
    q0_ref, q1_ref, q2_ref, k_ref, v_ref, zb_ref, qg_ref, kg_ref, inv_ref = refs[:9]
    yb_ref, ko_ref, vo_ref = refs[9 + n_alias:12 + n_alias]
    cos_ref, sin_ref, band_ref, causal_ref, qs_ref, og_ref, lg_ref = refs[12 + n_alias:]
    rows = 256
    n_chunks = seq // rows

    @pl.when((pl.program_id(0) == 0) & (pl.program_id(1) == 0))
    def _():
        def body(c, carry):
            r0 = pl.multiple_of(c * rows, rows)
            pos = (lax.broadcasted_iota(jnp.int32, (rows, LANES), 0) + r0).astype(_F32)
            lane = lax.broadcasted_iota(jnp.int32, (rows, LANES), 1)
            cos_t, sin_t = _rope_tables(pos, lane, inv_ref[...])
            cos_ref[pl.ds(r0, rows), :] = cos_t
            sin_ref[pl.ds(r0, rows), :] = sin_t
            return carry
        lax.fori_loop(0, n_chunks, body, 0)
        ri = lax.broadcasted_iota(jnp.int32, (QB, 2 * QB), 0)
        ci = lax.broadcasted_iota(jnp.int32, (QB, 2 * QB), 1)
        band_ref[...] = jnp.where(ci < ri, NEG, jnp.where(ci > ri + QB, NEG, 0.0))
        rq = lax.broadcasted_iota(jnp.int32, (QB, QB), 0)
        cq = lax.broadcasted_iota(jnp.int32, (QB, QB), 1)
        causal_ref[...] = jnp.where(cq > rq, NEG, 0.0)

    scale = HEAD_DIM ** -0.5
    rot_m = _rotate_half_matrix().astype(_BF16)
    rot2 = jnp.concatenate([rot_m, rot_m], axis=0)
    ones2 = jnp.ones((2 * LANES, LANES), _BF16)

    def norm_rope(x, g, cos_t, sin_t):
        ssq = _dot_hi_lo(x * x, ones2)
        y = x * lax.rsqrt(ssq * (1.0 / HEAD_DIM) + EPS) * g
        return y * cos_t + _dot_hi_lo(y, rot2) * sin_t

    def prep(c, carry):
        r0 = pl.multiple_of(c * rows, rows)
        sl = pl.ds(r0, rows)
        cos_t, sin_t = cos_ref[sl, :], sin_ref[sl, :]
        for g, q_ref in enumerate((q0_ref, q1_ref, q2_ref)):
            qs_ref[g, sl, :] = norm_rope(q_ref[0, sl, :], qg_ref[...], cos_t, sin_t) * scale
        ko_ref[0, 0, sl, :] = norm_rope(k_ref[0, sl, :], kg_ref[...], cos_t, sin_t)
        vo_ref[0, 0, sl, :] = v_ref[0, sl, :]
        for l2 in range(1, ko_ref.shape[0]):
            ko_ref[l2, 0, sl, :] = jnp.zeros((rows, LANES), _F32)
            vo_ref[l2, 0, sl, :] = jnp.zeros((rows, LANES), _F32)
        return carry
    lax.fori_loop(0, n_chunks, prep, 0, unroll=True)

    def rows_of(start, n, d):
        return pl.ds(start, n) if d == 1 else pl.ds(start, n, stride=d)

    for g, (w, d) in enumerate(GROUPS):
        assert w // d == QB
        n_blk = seq // (d * QB)
        for r in range(d):
            for blk in range(n_blk):
                q = qs_ref[g, rows_of(r + d * QB * blk, QB, d), :]
                if blk == 0:
                    ksl, bias = rows_of(r, QB, d), causal_ref[...]
                else:
                    ksl, bias = rows_of(r + d * QB * (blk - 1), 2 * QB, d), band_ref[...]
                o, lse = _attn_block(q, ko_ref[0, 0, ksl, :], v_ref[0, ksl, :], bias)
                osl = rows_of(r + d * QB * blk, QB, d)
                og_ref[g, osl, :] = o
                lg_ref[g, osl, :] = lse

    def merge(c, carry):
        r0 = pl.multiple_of(c * rows, rows)
        sl = pl.ds(r0, rows)
        l0, l1, l2 = lg_ref[0, sl, :], lg_ref[1, sl, :], lg_ref[2, sl, :]
        mx = jnp.maximum(jnp.maximum(l0, l1), l2)
        w0, w1, w2 = jnp.exp(l0 - mx), jnp.exp(l1 - mx), jnp.exp(l2 - mx)
        o = (w0 * og_ref[0, sl, :] + w1 * og_ref[1, sl, :] + w2 * og_ref[2, sl, :]) / (w0 + w1 + w2)
        yb_ref[0, sl, :] = (o * _silu(zb_ref[0, sl, :])).astype(_BF16)
        return carry
    lax.fori_loop(0, n_chunks, merge, 0)


def _attn_prompt(h3, qg, kg, inv, layer, kv_bufs):
    b, seq, _ = h3.shape
    hb = BR_W // LANES
    first = kv_bufs is None
    assert first == (layer == 0)

    def col(cb):
        return pl.BlockSpec((1, seq, LANES), lambda i, j: (i, 0, cb * hb + j))

    small = pl.BlockSpec((1, LANES), lambda i, j: (0, 0))
    any_spec = pl.BlockSpec(memory_space=pl.ANY)
    kv_out = pl.BlockSpec((DEPTH if first else 1, 1, seq, LANES), lambda i, j: (layer, i, 0, j))
    kv_shape = jax.ShapeDtypeStruct((DEPTH, b, seq, BR_W), _F32)
    n_alias = 0 if first else 2
    return pl.pallas_call(
        functools.partial(_attn_prompt_kernel, seq=seq, n_alias=n_alias),
        grid=(b, N_HEADS),
        in_specs=[col(COL_Q), col(COL_Q + 1), col(COL_Q + 2), col(COL_K), col(COL_V), col(COL_ZB),
                  small, small, small] + [any_spec] * n_alias,
        out_specs=[pl.BlockSpec((1, seq, LANES), lambda i, j: (i, 0, j)), kv_out, kv_out],
        out_shape=[jax.ShapeDtypeStruct((b, seq, BR_W), _BF16), kv_shape, kv_shape],
        input_output_aliases={} if first else {9: 1, 10: 2},
        scratch_shapes=[pltpu.VMEM((seq, LANES), _F32)] * 2
        + [pltpu.VMEM((QB, 2 * QB), _F32), pltpu.VMEM((QB, QB), _F32)]
        + [pltpu.VMEM((N_GROUPS, seq, LANES), _F32)] * 3,
        compiler_params=_params(("arbitrary", "arbitrary")),
        name="attn_prompt",
    )(h3, h3, h3, h3, h3, h3, qg, kg, inv, *(() if first else kv_bufs))


def _attn_sample_kernel(q0_ref, q1_ref, q2_ref, k_ref, v_ref, zb_ref, qg_ref, kg_ref, inv_ref,
                        kres_ref, krec_ref, vres_ref, vrec_ref, kbuf_hbm, vbuf_hbm,
                        yb_ref, ok_hbm, ov_hbm, kn_ref, vn_ref, sem, *, t_len, n_rows, layer):
    b = pl.program_id(0)
    shape = (t_len, N_HEADS, LANES)
    pos = (lax.broadcasted_iota(jnp.int32, shape, 0) + PAST_LEN).astype(_F32)
    lane = lax.broadcasted_iota(jnp.int32, shape, 2)
    cos_t, sin_t = _rope_tables(pos, lane, inv_ref[...].reshape(1, 1, LANES))

    def norm_rope(x, g_ref):
        y = _rms_heads(x, g_ref[...].reshape(1, 1, LANES))
        return y * cos_t + _rotate_half_lanes(y) * sin_t

    scale = HEAD_DIM ** -0.5
    k_new = norm_rope(k_ref[0], kg_ref)
    v_new = v_ref[0]
    kn_ref[...] = k_new
    vn_ref[...] = v_new
    nj = n_rows // 16
    tail = pl.ds(16 - t_len, t_len)
    copies = [pltpu.make_async_copy(kn_ref, ok_hbm.at[layer, b, nj - 1, tail], sem.at[0]),
              pltpu.make_async_copy(vn_ref, ov_hbm.at[layer, b, nj - 1, tail], sem.at[1])]
    for cp in copies:
        cp.start()

    qs = [norm_rope(q_ref[0], qg_ref) * scale for q_ref in (q0_ref, q1_ref, q2_ref)]

    for t in range(t_len):
        outs, lses = [], []
        for g, (w, d) in enumerate(GROUPS):
            q = qs[g][t]
            pieces = []
            nrec = krec_ref.shape[2]
            if d == 1:
                lo = nrec - QB // 16
                kc = krec_ref[0, 0, lo:nrec].reshape(QB, N_HEADS, LANES)
                vc = vrec_ref[0, 0, lo:nrec].reshape(QB, N_HEADS, LANES)
                idx = lax.broadcasted_iota(jnp.int32, (QB, N_HEADS, 1), 0)
                pieces.append((kc, vc, idx >= t))
                pieces.append((k_new[:t + 1], v_new[:t + 1], None))
            elif d < 16:
                per = 16 // d
                lo = nrec - (w // d) // per
                for m in range(per):
                    pieces.append((krec_ref[0, 0, lo:nrec, t + d * m], vrec_ref[0, 0, lo:nrec, t + d * m], None))
                pieces.append((k_new[t:t + 1], v_new[t:t + 1], None))
            else:
                assert d == 16
                lo = nj - w // 16
                pieces.append((kres_ref[0, 0, lo:nj, t], vres_ref[0, 0, lo:nj, t], None))
                pieces.append((k_new[t:t + 1], v_new[t:t + 1], None))
            scores = []
            for kk, vv, mask in pieces:
                s = jnp.sum(kk * q[None], axis=-1, keepdims=True)
                if mask is not None:
                    s = jnp.where(mask, s, NEG)
                scores.append(s)
            mx = functools.reduce(jnp.maximum, [jnp.max(s, axis=0) for s in scores])
            den = 0.0
            acc = 0.0
            for s, (kk, vv, mask) in zip(scores, pieces):
                p = jnp.exp(s - mx[None])
                den = den + jnp.sum(p, axis=0)
                acc = acc + jnp.sum(p * vv, axis=0)
            outs.append(acc / den)
            lses.append(mx + jnp.log(den))
        mx = jnp.maximum(jnp.maximum(lses[0], lses[1]), lses[2])
        ws = [jnp.exp(l - mx) for l in lses]
        o = (ws[0] * outs[0] + ws[1] * outs[1] + ws[2] * outs[2]) / (ws[0] + ws[1] + ws[2])
        yb_ref[0, t] = o * _silu(zb_ref[0, t])

    for cp in copies:
        cp.wait()


def _attn_sample(h4, qg, kg, inv, cache_k, cache_v, layer, kv_bufs):
    b, t_len = h4.shape[:2]
    n_rows = cache_k.shape[2]
    kc = cache_k.reshape(DEPTH, b, n_rows // 16, 16, N_HEADS, LANES)
    vc = cache_v.reshape(DEPTH, b, n_rows // 16, 16, N_HEADS, LANES)

    def col(cb):
        return pl.BlockSpec((1, t_len, N_HEADS, LANES), lambda i: (i, 0, cb, 0))

    small = pl.BlockSpec((1, LANES), lambda i: (0, 0))
    nj = n_rows // 16
    nrec = SAMPLE_RECENT // 16
    assert nj % nrec == 0 and 16 % t_len == 0 and all(w <= SAMPLE_RECENT for w, d in GROUPS if d < 16)
    resid = pl.BlockSpec((1, 1, nj, t_len, N_HEADS, LANES), lambda i: (layer, i, 0, 0, 0, 0))
    recent = pl.BlockSpec((1, 1, nrec, 16, N_HEADS, LANES), lambda i: (layer, i, nj // nrec - 1, 0, 0, 0))
    any_spec = pl.BlockSpec(memory_space=pl.ANY)
    rows_shape = (t_len, N_HEADS, LANES)
    return pl.pallas_call(
        functools.partial(_attn_sample_kernel, t_len=t_len, n_rows=n_rows, layer=layer),
        grid=(b,),
        in_specs=[col(COL_Q), col(COL_Q + 1), col(COL_Q + 2), col(COL_K), col(COL_V), col(COL_ZB),
                  small, small, small, resid, recent, resid, recent, any_spec, any_spec],
        out_specs=[pl.BlockSpec((1, t_len, N_HEADS, LANES), lambda i: (i, 0, 0, 0)), any_spec, any_spec],
        out_shape=[jax.ShapeDtypeStruct((b, t_len, N_HEADS, LANES), _F32),
                   jax.ShapeDtypeStruct(kc.shape, _F32), jax.ShapeDtypeStruct(vc.shape, _F32)],
        input_output_aliases={13: 1, 14: 2},
        scratch_shapes=[pltpu.VMEM(rows_shape, _F32)] * 2 + [pltpu.SemaphoreType.DMA((2,))],
        compiler_params=_params(("arbitrary",)),
        name="attn_sample",
    )(h4, h4, h4, h4, h4, h4, qg, kg, inv, kc, kc, vc, vc, *kv_bufs)


def _branch_kernel(va_ref, ca_ref, ba_ref, za_ref, uc_ref, zc_ref, ga_ref, gb_ref, zd_ref,
                   sta_ref, stc_ref, std_ref, aw_ref, pw_ref, cs_ref, dw_ref, db_ref, lg_ref, lb_ref,
                   ya_ref, yc_ref, yd_ref, na_ref, nc_ref, nd_ref,
                   eas_ref, ec_ref, eds_ref, pp_ref, *, tm, pos0):
    i = pl.program_id(1)
    ea_ref = eas_ref.at[0]
    ed_ref = eds_ref.at[0]
    a_offs = [HALO - (A_CONV - 1) + k for k in range(A_CONV)]
    a_shifts = sorted({off % 8 for off in a_offs} | {0})
    assert len(a_shifts) <= eas_ref.shape[0]

    @pl.when(i == 0)
    def _():
        ea_ref[0:HALO, :] = sta_ref[0]
        ec_ref[0:HALO, :] = stc_ref[0]
        ed_ref[0:HALO, :] = std_ref[0]
        pp_ref[...] = jnp.zeros_like(pp_ref)

    @pl.when(i > 0)
    def _():
        for e_ref in (ea_ref, ec_ref, ed_ref):
            e_ref[0:HALO, :] = e_ref[tm:tm + HALO, :]

    new = pl.ds(HALO, tm)
    ea_ref[new, :] = ca_ref[0] * va_ref[0]
    ec_ref[new, :] = uc_ref[0]
    ed_ref[new, :] = ga_ref[0] * _sigmoid(gb_ref[0])
    n_shift = HALO + tm - 8
    for r in range(1, 8):
        eds_ref[r, 0:n_shift, :] = ed_ref[pl.ds(r, n_shift), :]
    for slot, r in enumerate(a_shifts):
        if r:
            eas_ref[slot, 0:n_shift, :] = ea_ref[pl.ds(r, n_shift), :]

    pos = pos0 + i * tm + lax.broadcasted_iota(jnp.int32, (tm, C_GROUP), 0)
    for g, w in enumerate(POOL_WINDOWS):
        cols = slice(g * C_GROUP, (g + 1) * C_GROUP)
        tok = ec_ref[new, cols]
        s = tok
        for j in range(1, w):
            s = s + ec_ref[pl.ds(HALO - j, tm), cols]
        cnt = jnp.minimum(pos + 1, w).astype(_F32)
        pp_ref[0:tm, :] = s / cnt - tok
        y = jnp.dot(pp_ref[...].astype(_BF16), pw_ref[g], preferred_element_type=_F32)[0:tm]
        yc_ref[0, :, cols] = ((y * cs_ref[:, cols]) * _silu(zc_ref[0, :, cols])).astype(_BF16)

    rc = min(tm, 32)

    def conv_rows(w_ref, e_ref, slot_of, offs, r0):
        sub = min(rc, 8)
        accs = [None] * (rc // sub)
        for k, off in enumerate(offs):
            w = w_ref[k, 0:sub, :]
            for n in range(rc // sub):
                term = w * e_ref[slot_of(off % 8), pl.ds(off // 8 * 8 + r0 + sub * n, sub), :]
                accs[n] = term if accs[n] is None else accs[n] + term
        return accs[0] if len(accs) == 1 else jnp.concatenate(accs, axis=0)

    def chunk(r0):
        rows = pl.ds(r0, rc)
        conv = conv_rows(aw_ref, eas_ref, a_shifts.index, a_offs, r0)
        ya_ref[0, rows, :] = (ba_ref[0, rows, :] * conv * _silu(za_ref[0, rows, :])).astype(_BF16)
        d_offs = [HALO - (D_CONV - 1) + k for k in range(D_CONV)]
        x = conv_rows(dw_ref, eds_ref, lambda r: r, d_offs, r0) + db_ref[...]
        mu = jnp.mean(x, axis=-1, keepdims=True)
        xc = x - mu
        var = jnp.mean(xc * xc, axis=-1, keepdims=True)
        y = xc * lax.rsqrt(var + EPS) * lg_ref[...] + lb_ref[...]
        yd_ref[0, rows, :] = (_silu(y) * _silu(zd_ref[0, rows, :])).astype(_BF16)

    if tm == rc:
        chunk(0)
    else:
        def body(c, carry):
            chunk(pl.multiple_of(c * rc, rc))
            return carry
        lax.fori_loop(0, tm // rc, body, 0)

    @pl.when(i == pl.num_programs(1) - 1)
    def _():
        na_ref[0] = ea_ref[pl.ds(HALO + tm - (A_CONV - 1), A_CONV - 1), :]
        nc_ref[0] = ec_ref[pl.ds(HALO + tm - POOL_PAST, POOL_PAST), :]
        nd_ref[0] = ed_ref[pl.ds(HALO + tm - (D_CONV - 1), D_CONV - 1), :]


def _branches(h3, states, wl, pos0, tm):
    b, t_len, _ = h3.shape

    def col(cb):
        return pl.BlockSpec((1, tm, BR_W), lambda i, j: (i, j, cb))

    def full(shape):
        return pl.BlockSpec(shape, lambda i, j: (0,) * len(shape))

    state = pl.BlockSpec((1, HALO, BR_W), lambda i, j: (i, 0, 0))
    y_out = pl.BlockSpec((1, tm, BR_W), lambda i, j: (i, j, 0))

    def st_out(n):
        return pl.BlockSpec((1, n, BR_W), lambda i, j: (i, 0, 0))

    tmc = max(tm, 8)
    n_pool = len(POOL_WINDOWS)
    return pl.pallas_call(
        functools.partial(_branch_kernel, tm=tm, pos0=pos0),
        grid=(b, t_len // tm),
        in_specs=[col(COL_VA), col(COL_CA), col(COL_BA), col(COL_ZA), col(COL_UC), col(COL_ZC),
                  col(COL_GA), col(COL_GB), col(COL_ZD), state, state, state,
                  full((A_CONV, 8, BR_W)), full((n_pool, C_GROUP, C_GROUP)), full((1, BR_W)),
                  full((D_CONV, 8, BR_W)), full((1, BR_W)), full((1, BR_W)), full((1, BR_W))],
        out_specs=[y_out, y_out, y_out, st_out(A_CONV - 1), st_out(POOL_PAST), st_out(D_CONV - 1)],
        out_shape=[jax.ShapeDtypeStruct((b, t_len, BR_W), _BF16)] * 3
        + [jax.ShapeDtypeStruct((b, n, BR_W), _F32) for n in (A_CONV - 1, POOL_PAST, D_CONV - 1)],
        scratch_shapes=[pltpu.VMEM((A_CONV, HALO + tmc, BR_W), _F32), pltpu.VMEM((HALO + tmc, BR_W), _F32),
                        pltpu.VMEM((8, HALO + tmc, BR_W), _F32), pltpu.VMEM((tmc, C_GROUP), _F32)],
        compiler_params=_params(("arbitrary", "arbitrary")),
        name="branches",
    )(*([h3] * 9), *states, wl["a_conv_w"], wl["c_pool_w"], wl["c_scale"], wl["d_conv_w"],
      wl["d_conv_b"], wl["d_ln_g"], wl["d_ln_b"])


def _merge_kernel(ya_ref, yb_ref, yc_ref, yd_ref, wa_ref, wb_ref, wc_ref, wd_ref,
                  g0_ref, g1_ref, g2_ref, g3_ref, o_ref, *bf16_refs):
    acc = None
    for n, (y_ref, w_ref, g_ref) in enumerate(((ya_ref, wa_ref, g0_ref), (yb_ref, wb_ref, g1_ref),
                                               (yc_ref, wc_ref, g2_ref), (yd_ref, wd_ref, g3_ref))):
        w = w_ref[...].astype(_BF16)
        if bf16_refs:
            bf16_refs[n][...] = w
        term = _sigmoid(g_ref[...]) * jnp.dot(y_ref[...].astype(_BF16), w, preferred_element_type=_F32)
        acc = term if acc is None else acc + term
    o_ref[...] = acc.astype(_BF16)


def _merge(ys, w_brs, h2, tm, layer=None, tn=512):
    m = h2.shape[0]
    gate0 = COL_GATE * BR_W // tn
    y_spec = pl.BlockSpec((tm, BR_W), lambda i, j: (i, 0))
    w2d = pl.BlockSpec((BR_W, tn), lambda i, j: (0, j))
    w_spec = w2d if layer is None else pl.BlockSpec((None, BR_W, tn), lambda i, j: (layer, 0, j))
    out_specs = [pl.BlockSpec((tm, tn), lambda i, j: (i, j))]
    out_shape = [jax.ShapeDtypeStruct((m, D_MODEL), _BF16)]
    if layer is not None:
        assert m == tm
        out_specs += [w2d] * N_BRANCH
        out_shape += [jax.ShapeDtypeStruct((BR_W, D_MODEL), _BF16)] * N_BRANCH

    def gate(n):
        return pl.BlockSpec((tm, tn), lambda i, j: (i, gate0 + n * (D_MODEL // tn) + j))

    outs = pl.pallas_call(
        _merge_kernel,
        grid=(m // tm, D_MODEL // tn),
        in_specs=[y_spec] * 4 + [w_spec] * 4 + [gate(n) for n in range(N_BRANCH)],
        out_specs=out_specs,
        out_shape=out_shape,
        compiler_params=_params(("parallel", "arbitrary")),
        name="merge",
    )(*ys, *w_brs, h2, h2, h2, h2)
    return outs[0], tuple(outs[1:])


def _outproj_kernel(m_ref, w_ref, x_ref, o_ref, *bf16_refs):
    w = w_ref[...].astype(_BF16)
    if bf16_refs:
        bf16_refs[0][...] = w
    o_ref[...] = x_ref[...] + jnp.dot(m_ref[...], w, preferred_element_type=_F32)


def _outproj(merged, w_out, x2, tm, layer=None, tn=1024):
    m = x2.shape[0]
    w2d = pl.BlockSpec((D_MODEL, tn), lambda i, j: (0, j))
    w_spec = w2d if layer is None else pl.BlockSpec((None, D_MODEL, tn), lambda i, j: (layer, 0, j))
    out_specs = [pl.BlockSpec((tm, tn), lambda i, j: (i, j))]
    out_shape = [jax.ShapeDtypeStruct((m, D_MODEL), _F32)]
    if layer is not None:
        assert m == tm
        out_specs.append(w2d)
        out_shape.append(jax.ShapeDtypeStruct((D_MODEL, D_MODEL), _BF16))
    outs = pl.pallas_call(
        _outproj_kernel,
        grid=(m // tm, D_MODEL // tn),
        in_specs=[pl.BlockSpec((tm, D_MODEL), lambda i, j: (i, 0)), w_spec,
                  pl.BlockSpec((tm, tn), lambda i, j: (i, j))],
        out_specs=out_specs,
        out_shape=out_shape,
        compiler_params=_params(("parallel", "arbitrary")),
        name="outproj",
    )(merged, w_out, x2)
    return outs[0], (outs[1] if layer is not None else None)


def _layer(x3, h2, layer, past, wl, w_br, w_out, kv_bufs):
    b, t_len, _ = x3.shape
    m = b * t_len
    x2 = x3.reshape(m, D_MODEL)
    tm_mat = min(m, 1024)
    h3 = h2.reshape(b, t_len, N_IN)

    if past is None:
        yb, k_buf, v_buf = _attn_prompt(h3, wl["q_norm_g"], wl["k_norm_g"], wl["inv"], layer, kv_bufs)
        states = [jnp.zeros((b, HALO, BR_W), _F32)] * 3
        pos0, tm_br = 0, 256
    else:
        h4 = h2.reshape(b, t_len, N_IN // LANES, LANES)
        yb, k_buf, v_buf = _attn_sample(h4, wl["q_norm_g"], wl["k_norm_g"], wl["inv"], past[0], past[1],
                                        layer, kv_bufs)
        states = [jnp.pad(s[layer], ((0, 0), (HALO - s.shape[2], 0), (0, 0))) for s in past[2:]]
        pos0, tm_br = PAST_LEN, t_len
    yb = yb.reshape(m, BR_W)

    ya, yc, yd, new_a, new_c, new_d = _branches(h3, states, wl, pos0, tm_br)
    ys = (ya.reshape(m, BR_W), yb, yc.reshape(m, BR_W), yd.reshape(m, BR_W))
    w_layer = None if past is None else layer
    merged, w_br_bf16 = _merge(ys, w_br, h2, tm_mat, layer=w_layer)
    out, w_out_bf16 = _outproj(merged, w_out, x2, tm_mat, layer=w_layer)
    return out.reshape(b, t_len, D_MODEL), (k_buf, v_buf), (new_a, new_c, new_d), (w_br_bf16, w_out_bf16)


def kernel(x_prompt, x_sample, cache_attn_k, cache_attn_v, state_conv_a, state_pool_c, state_conv_d,
           norm_g, w_in, q_norm_g, k_norm_g, a_conv_w, c_pool_w, c_scale, d_conv_w, d_conv_b,
           d_ln_g, d_ln_b, w_br_a, w_br_b, w_br_c, w_br_d, w_out):
    half = ROT_DIM // 2
    inv = ROPE_THETA ** (-(jnp.arange(half, dtype=_F32) / half))
    inv = jnp.concatenate([inv, inv, jnp.zeros((LANES - ROT_DIM,), _F32)])[None, :]
    pool_w = c_pool_w.astype(_BF16)
    past = (cache_attn_k, cache_attn_v, state_conv_a, state_pool_c, state_conv_d)

    def on_sublanes(w):
        return jnp.broadcast_to(w[:, None, :], (w.shape[0], 8, w.shape[1]))

    hp, hs = x_prompt, x_sample
    kv_p = kv_s = None
    st_p = [[] for _ in range(3)]
    st_s = [[] for _ in range(3)]
    for l in range(DEPTH):
        wl = {"norm_g": norm_g[l][None], "q_norm_g": q_norm_g[l][None], "k_norm_g": k_norm_g[l][None],
              "inv": inv, "a_conv_w": on_sublanes(a_conv_w[l]), "c_pool_w": pool_w[l], "c_scale": c_scale[l][None],
              "d_conv_w": on_sublanes(d_conv_w[l]), "d_conv_b": d_conv_b[l][None], "d_ln_g": d_ln_g[l][None],
              "d_ln_b": d_ln_b[l][None]}
        h_s, w_in_bf16 = _inproj_cast(hs.reshape(-1, D_MODEL), wl["norm_g"], w_in, l)
        if l == 0:
            h_p, *kv_s = _inproj(hp.reshape(-1, D_MODEL), wl["norm_g"], w_in_bf16, 1024,
                                 caches=(cache_attn_k, cache_attn_v), t_len=x_sample.shape[1])
        else:
            h_p = _inproj(hp.reshape(-1, D_MODEL), wl["norm_g"], w_in_bf16, 1024)
        hs, kv_s, ss, (w_br_bf16, w_out_bf16) = _layer(hs, h_s, l, past, wl, (w_br_a, w_br_b, w_br_c, w_br_d),
                                                       w_out, kv_s)
        hp, kv_p, sp, _ = _layer(hp, h_p, l, None, wl, w_br_bf16, w_out_bf16, kv_p)
        for n in range(3):
            st_p[n].append(sp[n])
            st_s[n].append(ss[n])
    b, seq = x_prompt.shape[:2]
    kv_p = [a.reshape(DEPTH, b, seq, N_HEADS, HEAD_DIM) for a in kv_p]
    kv_s = [a.reshape(cache_attn_k.shape) for a in kv_s]
    return (hp, hs, *kv_p, *(jnp.stack(s, axis=0) for s in st_p),
            *kv_s, *(jnp.stack(s, axis=0) for s in st_s))
```

```python
import functools

import jax
import jax.numpy as jnp
from jax import lax
from jax.experimental import pallas as pl
from jax.experimental.pallas import tpu as pltpu

D_MODEL = 2048
DEPTH = 2
PAST_LEN = 16384
BR_W = D_MODEL // 2
N_BRANCH = 4
A_CONV = 3
HEAD_DIM = 128
N_HEADS = BR_W // HEAD_DIM
GROUPS = ((128, 1), (512, 4), (2048, 16))
N_GROUPS = len(GROUPS)
ROT_DIM = HEAD_DIM // 4
ROPE_THETA = 500000.0
POOL_WINDOWS = (2, 4, 8, 16)
C_GROUP = BR_W // len(POOL_WINDOWS)
POOL_PAST = POOL_WINDOWS[-1] - 1
D_CONV = 31
EPS = 1e-6
N_IN = 4 * BR_W + (N_GROUPS + 3) * BR_W + 2 * BR_W + 3 * BR_W + N_BRANCH * D_MODEL

COL_VA, COL_CA, COL_BA, COL_ZA = 0, 1, 2, 3
COL_Q, COL_K, COL_V, COL_ZB = 4, 7, 8, 9
COL_UC, COL_ZC = 10, 11
COL_GA, COL_GB, COL_ZD = 12, 13, 14
COL_GATE = 15

LANES = 128
QB = 128
SAMPLE_RECENT = 512
HALO = 32
NEG = -1e30
VMEM_LIMIT = 56 * 1024 * 1024

_BF16 = jnp.bfloat16
_F32 = jnp.float32


def _sigmoid(x):
    return 0.5 * jnp.tanh(0.5 * x) + 0.5


def _silu(x):
    return x * _sigmoid(x)


def _params(sem):
    return pltpu.CompilerParams(dimension_semantics=sem, vmem_limit_bytes=VMEM_LIMIT)


def _shift_block(src, nxt, dst, last, t_len):
    nj = src.shape[2]
    keep = 16 - t_len
    dst[0, 0, :, 0:keep] = src[0, 0, :, t_len:16]
    dst[0, 0, 0:nj - 1, keep:16] = src[0, 0, 1:nj, 0:t_len]
    dst[0, 0, nj - 1, keep:16] = jnp.where(last, 0.0, nxt[0, 0, 0, 0:t_len])


def _inproj_kernel(x_ref, g_ref, w_ref, *rest, shift):
    if shift is None:
        h_ref, xn_ref = rest
    else:
        k_ref, kn_ref, v_ref, vn_ref, h_ref, ok_ref, ov_ref, xn_ref = rest
    j = pl.program_id(1)

    @pl.when(j == 0)
    def _():
        x = x_ref[...]
        y = x * lax.rsqrt(jnp.mean(x * x, axis=-1, keepdims=True) + EPS)
        xn_ref[...] = (y * g_ref[...]).astype(_BF16)

    if shift is not None:
        t_len, per_layer, n_steps = shift
        last = lax.rem(jnp.minimum(j, n_steps - 1), per_layer) == per_layer - 1
        _shift_block(k_ref, kn_ref, ok_ref, last, t_len)
        _shift_block(v_ref, vn_ref, ov_ref, last, t_len)

    h_ref[...] = jnp.dot(xn_ref[...], w_ref[...], preferred_element_type=_F32)


def _inproj(x2, g, w_bf16, tm, caches=None, t_len=None, tn=1024, nj_blk=16):
    m = x2.shape[0]
    n_i, n_j = m // tm, N_IN // tn
    in_specs = [
        pl.BlockSpec((tm, D_MODEL), lambda i, j: (i, 0)),
        pl.BlockSpec((1, D_MODEL), lambda i, j: (0, 0)),
        pl.BlockSpec((D_MODEL, tn), lambda i, j: (0, j)),
    ]
    out_specs = [pl.BlockSpec((tm, tn), lambda i, j: (i, j))]
    out_shape = [jax.ShapeDtypeStruct((m, N_IN), _F32)]
    operands = [x2, g, w_bf16]
    shift = None
    if caches is not None:
        depth, b, n_rows = caches[0].shape[:3]
        nj = n_rows // 16
        per_layer = nj // nj_blk
        n_steps = depth * per_layer
        assert b == n_i and n_steps <= n_j and t_len < 16 and nj % nj_blk == 0
        view = (depth, b, nj, 16, N_HEADS, LANES)

        def where(j):
            s = jnp.minimum(j, n_steps - 1)
            return s // per_layer, s % per_layer

        main = pl.BlockSpec((1, 1, nj_blk, 16, N_HEADS, LANES),
                            lambda i, j: (where(j)[0], i, where(j)[1], 0, 0, 0))
        nxt = pl.BlockSpec((1, 1, 1, 16, N_HEADS, LANES),
                           lambda i, j: (where(j)[0], i, jnp.minimum((where(j)[1] + 1) * nj_blk, nj - 1), 0, 0, 0))
        in_specs += [main, nxt, main, nxt]
        out_specs += [main, main]
        out_shape += [jax.ShapeDtypeStruct(view, _F32)] * 2
        kc, vc = caches[0].reshape(view), caches[1].reshape(view)
        operands += [kc, kc, vc, vc]
        shift = (t_len, per_layer, n_steps)
    outs = pl.pallas_call(
        functools.partial(_inproj_kernel, shift=shift),
        grid=(n_i, n_j),
        in_specs=in_specs,
        out_specs=out_specs,
        out_shape=out_shape,
        scratch_shapes=[pltpu.VMEM((tm, D_MODEL), _BF16)],
        compiler_params=_params(("parallel", "arbitrary")),
        name="inproj",
    )(*operands)
    return outs[0] if caches is None else outs


def _inproj_cast_kernel(x_ref, g_ref, w_ref, h_ref, wb_ref, xn_ref):
    @pl.when(pl.program_id(0) == 0)
    def _():
        x = x_ref[...]
        y = x * lax.rsqrt(jnp.mean(x * x, axis=-1, keepdims=True) + EPS)
        xn_ref[...] = (y * g_ref[...]).astype(_BF16)

    wb = w_ref[...].astype(_BF16)
    wb_ref[...] = wb
    h_ref[...] = jnp.dot(xn_ref[...], wb, preferred_element_type=_F32)


def _inproj_cast(x2, g, w_in, layer, tn=1024):
    m = x2.shape[0]
    return pl.pallas_call(
        _inproj_cast_kernel,
        grid=(N_IN // tn,),
        in_specs=[
            pl.BlockSpec((m, D_MODEL), lambda j: (0, 0)),
            pl.BlockSpec((1, D_MODEL), lambda j: (0, 0)),
            pl.BlockSpec((None, D_MODEL, tn), lambda j: (layer, 0, j)),
        ],
        out_specs=[pl.BlockSpec((m, tn), lambda j: (0, j)), pl.BlockSpec((D_MODEL, tn), lambda j: (0, j))],
        out_shape=[jax.ShapeDtypeStruct((m, N_IN), _F32), jax.ShapeDtypeStruct((D_MODEL, N_IN), _BF16)],
        scratch_shapes=[pltpu.VMEM((m, D_MODEL), _BF16)],
        compiler_params=_params(("arbitrary",)),
        name="inproj_cast",
    )(x2, g, w_in)


def _rope_tables(pos, lane, inv):
    ang = pos * inv
    cos_t = jnp.where(lane < ROT_DIM, jnp.cos(ang), 1.0)
    sin_t = jnp.where(lane < ROT_DIM, jnp.sin(ang), 0.0)
    return cos_t, sin_t


def _rotate_half_matrix():
    half = ROT_DIM // 2
    m = lax.broadcasted_iota(jnp.int32, (LANES, LANES), 0)
    l = lax.broadcasted_iota(jnp.int32, (LANES, LANES), 1)
    return jnp.where(m == l + half, jnp.where(l < half, -1.0, 0.0),
                     jnp.where(m == l - half, jnp.where(l < ROT_DIM, 1.0, 0.0), 0.0))


def _rotate_half_lanes(x):
    ax = x.ndim - 1
    half = ROT_DIM // 2
    lane = lax.broadcasted_iota(jnp.int32, x.shape, ax)
    return jnp.where(lane < half, -pltpu.roll(x, LANES - half, ax), pltpu.roll(x, half, ax))


def _dot_hi_lo(x, w2):
    hi = x.astype(_BF16)
    lo = (x - hi.astype(_F32)).astype(_BF16)
    return jnp.dot(jnp.concatenate([hi, lo], axis=1), w2, preferred_element_type=_F32)


def _rms_heads(x, g):
    return x * lax.rsqrt(jnp.mean(x * x, axis=-1, keepdims=True) + EPS) * g


def _attn_block(q, k, v, bias):
    s = lax.dot_general(q.astype(_BF16), k.astype(_BF16), (((1,), (1,)), ((), ())),
                        preferred_element_type=_F32) + bias
    m = jnp.max(s, axis=-1, keepdims=True)
    p = jnp.exp(s - m).astype(_BF16)
    v1 = jnp.concatenate([v.astype(_BF16), jnp.ones(v.shape, _BF16)], axis=1)
    acc = jnp.dot(p, v1, preferred_element_type=_F32)
    l = acc[:, LANES:]
    return acc[:, :LANES] * (1.0 / l), m + jnp.log(l)


def _attn_prompt_kernel(*refs, seq, n_alias):
    q0_ref, q1_ref, q2_ref, k_ref, v_ref, zb_ref, qg_ref, kg_ref, inv_ref = refs[:9]
    yb_ref, ko_ref, vo_ref = refs[9 + n_alias:12 + n_alias]
    cos_ref, sin_ref, band_ref, causal_ref, qs_ref, og_ref, lg_ref = refs[12 + n_alias:]
    rows = 256
    n_chunks = seq // rows

    @pl.when((pl.program_id(0) == 0) & (pl.program_id(1) == 0))
    def _():
        def body(c, carry):
            r0 = pl.multiple_of(c * rows, rows)
            pos = (lax.broadcasted_iota(jnp.int32, (rows, LANES), 0) + r0).astype(_F32)
            lane = lax.broadcasted_iota(jnp.int32, (rows, LANES), 1)
            cos_t, sin_t = _rope_tables(pos, lane, inv_ref[...])
            cos_ref[pl.ds(r0, rows), :] = cos_t
            sin_ref[pl.ds(r0, rows), :] = sin_t
            return carry
        lax.fori_loop(0, n_chunks, body, 0)
        ri = lax.broadcasted_iota(jnp.int32, (QB, 2 * QB), 0)
        ci = lax.broadcasted_iota(jnp.int32, (QB, 2 * QB), 1)
        band_ref[...] = jnp.where(ci < ri, NEG, jnp.where(ci > ri + QB, NEG, 0.0))
        rq = lax.broadcasted_iota(jnp.int32, (QB, QB), 0)
        cq = lax.broadcasted_iota(jnp.int32, (QB, QB), 1)
        causal_ref[...] = jnp.where(cq > rq, NEG, 0.0)

    scale = HEAD_DIM ** -0.5
    rot_m = _rotate_half_matrix().astype(_BF16)
    rot2 = jnp.concatenate([rot_m, rot_m], axis=0)
    ones2 = jnp.ones((2 * LANES, LANES), _BF16)

    def norm_rope(x, g, cos_t, sin_t):
        ssq = _dot_hi_lo(x * x, ones2)
        y = x * lax.rsqrt(ssq * (1.0 / HEAD_DIM) + EPS) * g
        return y * cos_t + _dot_hi_lo(y, rot2) * sin_t

    def prep(c, carry):
        r0 = pl.multiple_of(c * rows, rows)
        sl = pl.ds(r0, rows)
        cos_t, sin_t = cos_ref[sl, :], sin_ref[sl, :]
        for g, q_ref in enumerate((q0_ref, q1_ref, q2_ref)):
            qs_ref[g, sl, :] = norm_rope(q_ref[0, sl, :], qg_ref[...] * scale, cos_t, sin_t)
        ko_ref[0, 0, sl, :] = norm_rope(k_ref[0, sl, :], kg_ref[...], cos_t, sin_t)
        vo_ref[0, 0, sl, :] = v_ref[0, sl, :]
        for l2 in range(1, ko_ref.shape[0]):
            ko_ref[l2, 0, sl, :] = jnp.zeros((rows, LANES), _F32)
            vo_ref[l2, 0, sl, :] = jnp.zeros((rows, LANES), _F32)
        return carry
    lax.fori_loop(0, n_chunks, prep, 0, unroll=True)

    def rows_of(start, n, d):
        return pl.ds(start, n) if d == 1 else pl.ds(start, n, stride=d)

    for g, (w, d) in enumerate(GROUPS):
        assert w // d == QB
        n_blk = seq // (d * QB)
        for r in range(d):
            for blk in range(n_blk):
                q = qs_ref[g, rows_of(r + d * QB * blk, QB, d), :]
                if blk == 0:
                    ksl, bias = rows_of(r, QB, d), causal_ref[...]
                else:
                    ksl, bias = rows_of(r + d * QB * (blk - 1), 2 * QB, d), band_ref[...]
                o, lse = _attn_block(q, ko_ref[0, 0, ksl, :], v_ref[0, ksl, :], bias)
                osl = rows_of(r + d * QB * blk, QB, d)
                og_ref[g, osl, :] = o
                lg_ref[g, osl, :] = lse

    def merge(c, carry):
        r0 = pl.multiple_of(c * rows, rows)
        sl = pl.ds(r0, rows)
        l0, l1, l2 = lg_ref[0, sl, :], lg_ref[1, sl, :], lg_ref[2, sl, :]
        mx = jnp.maximum(jnp.maximum(l0, l1), l2)
        w0, w1, w2 = jnp.exp(l0 - mx), jnp.exp(l1 - mx), jnp.exp(l2 - mx)
        o = (w0 * og_ref[0, sl, :] + w1 * og_ref[1, sl, :] + w2 * og_ref[2, sl, :]) / (w0 + w1 + w2)
        yb_ref[0, sl, :] = (o * _silu(zb_ref[0, sl, :])).astype(_BF16)
        return carry
    lax.fori_loop(0, n_chunks, merge, 0)


def _attn_prompt(h3, qg, kg, inv, layer, kv_bufs):
    b, seq, _ = h3.shape
    hb = BR_W // LANES
    first = kv_bufs is None
    assert first == (layer == 0)

    def col(cb):
        return pl.BlockSpec((1, seq, LANES), lambda i, j: (i, 0, cb * hb + j))

    small = pl.BlockSpec((1, LANES), lambda i, j: (0, 0))
    any_spec = pl.BlockSpec(memory_space=pl.ANY)
    kv_out = pl.BlockSpec((DEPTH if first else 1, 1, seq, LANES), lambda i, j: (layer, i, 0, j))
    kv_shape = jax.ShapeDtypeStruct((DEPTH, b, seq, BR_W), _F32)
    n_alias = 0 if first else 2
    return pl.pallas_call(
        functools.partial(_attn_prompt_kernel, seq=seq, n_alias=n_alias),
        grid=(b, N_HEADS),
        in_specs=[col(COL_Q), col(COL_Q + 1), col(COL_Q + 2), col(COL_K), col(COL_V), col(COL_ZB),
                  small, small, small] + [any_spec] * n_alias,
        out_specs=[pl.BlockSpec((1, seq, LANES), lambda i, j: (i, 0, j)), kv_out, kv_out],
        out_shape=[jax.ShapeDtypeStruct((b, seq, BR_W), _BF16), kv_shape, kv_shape],
        input_output_aliases={} if first else {9: 1, 10: 2},
        scratch_shapes=[pltpu.VMEM((seq, LANES), _F32)] * 2
        + [pltpu.VMEM((QB, 2 * QB), _F32), pltpu.VMEM((QB, QB), _F32)]
        + [pltpu.VMEM((N_GROUPS, seq, LANES), _F32)] * 3,
        compiler_params=_params(("arbitrary", "arbitrary")),
        name="attn_prompt",
    )(h3, h3, h3, h3, h3, h3, qg, kg, inv, *(() if first else kv_bufs))


def _attn_sample_kernel(q0_ref, q1_ref, q2_ref, k_ref, v_ref, zb_ref, qg_ref, kg_ref, inv_ref,
                        kres_ref, krec_ref, vres_ref, vrec_ref, kbuf_hbm, vbuf_hbm,
                        yb_ref, ok_hbm, ov_hbm, kn_ref, vn_ref, sem, *, t_len, n_rows, layer):
    b = pl.program_id(0)
    shape = (t_len, N_HEADS, LANES)
    pos = (lax.broadcasted_iota(jnp.int32, shape, 0) + PAST_LEN).astype(_F32)
    lane = lax.broadcasted_iota(jnp.int32, shape, 2)
    cos_t, sin_t = _rope_tables(pos, lane, inv_ref[...].reshape(1, 1, LANES))

    def norm_rope(x, g_ref):
        y = _rms_heads(x, g_ref[...].reshape(1, 1, LANES))
        return y * cos_t + _rotate_half_lanes(y) * sin_t

    scale = HEAD_DIM ** -0.5
    k_new = norm_rope(k_ref[0], kg_ref)
    v_new = v_ref[0]
    kn_ref[...] = k_new
    vn_ref[...] = v_new
    nj = n_rows // 16
    tail = pl.ds(16 - t_len, t_len)
    copies = [pltpu.make_async_copy(kn_ref, ok_hbm.at[layer, b, nj - 1, tail], sem.at[0]),
              pltpu.make_async_copy(vn_ref, ov_hbm.at[layer, b, nj - 1, tail], sem.at[1])]
    for cp in copies:
        cp.start()

    qs = [norm_rope(q_ref[0], qg_ref) * scale for q_ref in (q0_ref, q1_ref, q2_ref)]

    for t in range(t_len):
        outs, lses = [], []
        for g, (w, d) in enumerate(GROUPS):
            q = qs[g][t]
            pieces = []
            nrec = krec_ref.shape[2]
            if d == 1:
                lo = nrec - QB // 16
                kc = krec_ref[0, 0, lo:nrec].reshape(QB, N_HEADS, LANES)
                vc = vrec_ref[0, 0, lo:nrec].reshape(QB, N_HEADS, LANES)
                idx = lax.broadcasted_iota(jnp.int32, (QB, N_HEADS, 1), 0)
                pieces.append((kc, vc, idx >= t))
                pieces.append((k_new[:t + 1], v_new[:t + 1], None))
            elif d < 16:
                per = 16 // d
                lo = nrec - (w // d) // per
                for m in range(per):
                    pieces.append((krec_ref[0, 0, lo:nrec, t + d * m], vrec_ref[0, 0, lo:nrec, t + d * m], None))
                pieces.append((k_new[t:t + 1], v_new[t:t + 1], None))
            else:
                assert d == 16
                lo = nj - w // 16
                pieces.append((kres_ref[0, 0, lo:nj, t], vres_ref[0, 0, lo:nj, t], None))
                pieces.append((k_new[t:t + 1], v_new[t:t + 1], None))
            scores = []
            for kk, vv, mask in pieces:
                s = jnp.sum(kk * q[None], axis=-1, keepdims=True)
                if mask is not None:
                    s = jnp.where(mask, s, NEG)
                scores.append(s)
            mx = functools.reduce(jnp.maximum, [jnp.max(s, axis=0) for s in scores])
            den = 0.0
            acc = 0.0
            for s, (kk, vv, mask) in zip(scores, pieces):
                p = jnp.exp(s - mx[None])
                den = den + jnp.sum(p, axis=0)
                acc = acc + jnp.sum(p * vv, axis=0)
            outs.append(acc / den)
            lses.append(mx + jnp.log(den))
        mx = jnp.maximum(jnp.maximum(lses[0], lses[1]), lses[2])
        ws = [jnp.exp(l - mx) for l in lses]
        o = (ws[0] * outs[0] + ws[1] * outs[1] + ws[2] * outs[2]) / (ws[0] + ws[1] + ws[2])
        yb_ref[0, t] = o * _silu(zb_ref[0, t])

    for cp in copies:
        cp.wait()


def _attn_sample(h4, qg, kg, inv, cache_k, cache_v, layer, kv_bufs):
    b, t_len = h4.shape[:2]
    n_rows = cache_k.shape[2]
    kc = cache_k.reshape(DEPTH, b, n_rows // 16, 16, N_HEADS, LANES)
    vc = cache_v.reshape(DEPTH, b, n_rows // 16, 16, N_HEADS, LANES)

    def col(cb):
        return pl.BlockSpec((1, t_len, N_HEADS, LANES), lambda i: (i, 0, cb, 0))

    small = pl.BlockSpec((1, LANES), lambda i: (0, 0))
    nj = n_rows // 16
    nrec = SAMPLE_RECENT // 16
    assert nj % nrec == 0 and 16 % t_len == 0 and all(w <= SAMPLE_RECENT for w, d in GROUPS if d < 16)
    resid = pl.BlockSpec((1, 1, nj, t_len, N_HEADS, LANES), lambda i: (layer, i, 0, 0, 0, 0))
    recent = pl.BlockSpec((1, 1, nrec, 16, N_HEADS, LANES), lambda i: (layer, i, nj // nrec - 1, 0, 0, 0))
    any_spec = pl.BlockSpec(memory_space=pl.ANY)
    rows_shape = (t_len, N_HEADS, LANES)
    return pl.pallas_call(
        functools.partial(_attn_sample_kernel, t_len=t_len, n_rows=n_rows, layer=layer),
        grid=(b,),
        in_specs=[col(COL_Q), col(COL_Q + 1), col(COL_Q + 2), col(COL_K), col(COL_V), col(COL_ZB),
                  small, small, small, resid, recent, resid, recent, any_spec, any_spec],
        out_specs=[pl.BlockSpec((1, t_len, N_HEADS, LANES), lambda i: (i, 0, 0, 0)), any_spec, any_spec],
        out_shape=[jax.ShapeDtypeStruct((b, t_len, N_HEADS, LANES), _F32),
                   jax.ShapeDtypeStruct(kc.shape, _F32), jax.ShapeDtypeStruct(vc.shape, _F32)],
        input_output_aliases={13: 1, 14: 2},
        scratch_shapes=[pltpu.VMEM(rows_shape, _F32)] * 2 + [pltpu.SemaphoreType.DMA((2,))],
        compiler_params=_params(("arbitrary",)),
        name="attn_sample",
    )(h4, h4, h4, h4, h4, h4, qg, kg, inv, kc, kc, vc, vc, *kv_bufs)


def _branch_kernel(va_ref, ca_ref, ba_ref, za_ref, uc_ref, zc_ref, ga_ref, gb_ref, zd_ref,
                   sta_ref, stc_ref, std_ref, aw_ref, pw_ref, cs_ref, dw_ref, db_ref, lg_ref, lb_ref,
                   ya_ref, yc_ref, yd_ref, na_ref, nc_ref, nd_ref,
                   eas_ref, ec_ref, eds_ref, pp_ref, *, tm, pos0):
    i = pl.program_id(1)
    ea_ref = eas_ref.at[0]
    ed_ref = eds_ref.at[0]
    a_offs = [HALO - (A_CONV - 1) + k for k in range(A_CONV)]
    a_shifts = sorted({off % 8 for off in a_offs} | {0})
    assert len(a_shifts) <= eas_ref.shape[0]

    @pl.when(i == 0)
    def _():
        ea_ref[0:HALO, :] = sta_ref[0]
        ec_ref[0:HALO, :] = stc_ref[0]
        ed_ref[0:HALO, :] = std_ref[0]
        pp_ref[...] = jnp.zeros_like(pp_ref)

    @pl.when(i > 0)
    def _():
        for e_ref in (ea_ref, ec_ref, ed_ref):
            e_ref[0:HALO, :] = e_ref[tm:tm + HALO, :]

    new = pl.ds(HALO, tm)
    ea_ref[new, :] = ca_ref[0] * va_ref[0]
    ec_ref[new, :] = uc_ref[0]
    ed_ref[new, :] = ga_ref[0] * _sigmoid(gb_ref[0])
    n_shift = HALO + tm - 8
    for r in range(1, 8):
        eds_ref[r, 0:n_shift, :] = ed_ref[pl.ds(r, n_shift), :]
    for slot, r in enumerate(a_shifts):
        if r:
            eas_ref[slot, 0:n_shift, :] = ea_ref[pl.ds(r, n_shift), :]

    pos = pos0 + i * tm + lax.broadcasted_iota(jnp.int32, (tm, C_GROUP), 0)
    for g, w in enumerate(POOL_WINDOWS):
        cols = slice(g * C_GROUP, (g + 1) * C_GROUP)
        tok = ec_ref[new, cols]
        s = tok
        for j in range(1, w):
            s = s + ec_ref[pl.ds(HALO - j, tm), cols]
        cnt = jnp.minimum(pos + 1, w).astype(_F32)
        pp_ref[0:tm, :] = s / cnt - tok
        y = jnp.dot(pp_ref[...].astype(_BF16), pw_ref[g], preferred_element_type=_F32)[0:tm]
        yc_ref[0, :, cols] = ((y * cs_ref[:, cols]) * _silu(zc_ref[0, :, cols])).astype(_BF16)

    rc = min(tm, 32)

    def conv_rows(w_ref, e_ref, slot_of, offs, r0):
        sub = min(rc, 8)
        accs = [None] * (rc // sub)
        for k, off in enumerate(offs):
            w = w_ref[k, 0:sub, :]
            for n in range(rc // sub):
                term = w * e_ref[slot_of(off % 8), pl.ds(off // 8 * 8 + r0 + sub * n, sub), :]
                accs[n] = term if accs[n] is None else accs[n] + term
        return accs[0] if len(accs) == 1 else jnp.concatenate(accs, axis=0)

    def chunk(r0):
        rows = pl.ds(r0, rc)
        conv = conv_rows(aw_ref, eas_ref, a_shifts.index, a_offs, r0)
        ya_ref[0, rows, :] = (ba_ref[0, rows, :] * conv * _silu(za_ref[0, rows, :])).astype(_BF16)
        d_offs = [HALO - (D_CONV - 1) + k for k in range(D_CONV)]
        x = conv_rows(dw_ref, eds_ref, lambda r: r, d_offs, r0) + db_ref[...]
        mu = jnp.mean(x, axis=-1, keepdims=True)
        xc = x - mu
        var = jnp.mean(xc * xc, axis=-1, keepdims=True)
        y = xc * lax.rsqrt(var + EPS) * lg_ref[...] + lb_ref[...]
        yd_ref[0, rows, :] = (_silu(y) * _silu(zd_ref[0, rows, :])).astype(_BF16)

    if tm == rc:
        chunk(0)
    else:
        def body(c, carry):
            chunk(pl.multiple_of(c * rc, rc))
            return carry
        lax.fori_loop(0, tm // rc, body, 0, unroll=2)

    @pl.when(i == pl.num_programs(1) - 1)
    def _():
        na_ref[0] = ea_ref[pl.ds(HALO + tm - (A_CONV - 1), A_CONV - 1), :]
        nc_ref[0] = ec_ref[pl.ds(HALO + tm - POOL_PAST, POOL_PAST), :]
        nd_ref[0] = ed_ref[pl.ds(HALO + tm - (D_CONV - 1), D_CONV - 1), :]


def _branches(h3, states, wl, pos0, tm):
    b, t_len, _ = h3.shape

    def col(cb):
        return pl.BlockSpec((1, tm, BR_W), lambda i, j: (i, j, cb))

    def full(shape):
        return pl.BlockSpec(shape, lambda i, j: (0,) * len(shape))

    state = pl.BlockSpec((1, HALO, BR_W), lambda i, j: (i, 0, 0))
    y_out = pl.BlockSpec((1, tm, BR_W), lambda i, j: (i, j, 0))

    def st_out(n):
        return pl.BlockSpec((1, n, BR_W), lambda i, j: (i, 0, 0))

    tmc = max(tm, 8)
    n_pool = len(POOL_WINDOWS)
    return pl.pallas_call(
        functools.partial(_branch_kernel, tm=tm, pos0=pos0),
        grid=(b, t_len // tm),
        in_specs=[col(COL_VA), col(COL_CA), col(COL_BA), col(COL_ZA), col(COL_UC), col(COL_ZC),
                  col(COL_GA), col(COL_GB), col(COL_ZD), state, state, state,
                  full((A_CONV, 8, BR_W)), full((n_pool, C_GROUP, C_GROUP)), full((1, BR_W)),
                  full((D_CONV, 8, BR_W)), full((1, BR_W)), full((1, BR_W)), full((1, BR_W))],
        out_specs=[y_out, y_out, y_out, st_out(A_CONV - 1), st_out(POOL_PAST), st_out(D_CONV - 1)],
        out_shape=[jax.ShapeDtypeStruct((b, t_len, BR_W), _BF16)] * 3
        + [jax.ShapeDtypeStruct((b, n, BR_W), _F32) for n in (A_CONV - 1, POOL_PAST, D_CONV - 1)],
        scratch_shapes=[pltpu.VMEM((A_CONV, HALO + tmc, BR_W), _F32), pltpu.VMEM((HALO + tmc, BR_W), _F32),
                        pltpu.VMEM((8, HALO + tmc, BR_W), _F32), pltpu.VMEM((tmc, C_GROUP), _F32)],
        compiler_params=_params(("arbitrary", "arbitrary")),
        name="branches",
    )(*([h3] * 9), *states, wl["a_conv_w"], wl["c_pool_w"], wl["c_scale"], wl["d_conv_w"],
      wl["d_conv_b"], wl["d_ln_g"], wl["d_ln_b"])


def _merge_kernel(ya_ref, yb_ref, yc_ref, yd_ref, wa_ref, wb_ref, wc_ref, wd_ref,
                  g0_ref, g1_ref, g2_ref, g3_ref, o_ref, *bf16_refs):
    acc = None
    for n, (y_ref, w_ref, g_ref) in enumerate(((ya_ref, wa_ref, g0_ref), (yb_ref, wb_ref, g1_ref),
                                               (yc_ref, wc_ref, g2_ref), (yd_ref, wd_ref, g3_ref))):
        w = w_ref[...].astype(_BF16)
        if bf16_refs:
            bf16_refs[n][...] = w
        term = _sigmoid(g_ref[...]) * jnp.dot(y_ref[...].astype(_BF16), w, preferred_element_type=_F32)
        acc = term if acc is None else acc + term
    o_ref[...] = acc.astype(_BF16)


def _merge(ys, w_brs, h2, tm, layer=None, tn=512):
    m = h2.shape[0]
    gate0 = COL_GATE * BR_W // tn
    y_spec = pl.BlockSpec((tm, BR_W), lambda i, j: (i, 0))
    w2d = pl.BlockSpec((BR_W, tn), lambda i, j: (0, j))
    w_spec = w2d if layer is None else pl.BlockSpec((None, BR_W, tn), lambda i, j: (layer, 0, j))
    out_specs = [pl.BlockSpec((tm, tn), lambda i, j: (i, j))]
    out_shape = [jax.ShapeDtypeStruct((m, D_MODEL), _BF16)]
    if layer is not None:
        assert m == tm
        out_specs += [w2d] * N_BRANCH
        out_shape += [jax.ShapeDtypeStruct((BR_W, D_MODEL), _BF16)] * N_BRANCH

    def gate(n):
        return pl.BlockSpec((tm, tn), lambda i, j: (i, gate0 + n * (D_MODEL // tn) + j))

    outs = pl.pallas_call(
        _merge_kernel,
        grid=(m // tm, D_MODEL // tn),
        in_specs=[y_spec] * 4 + [w_spec] * 4 + [gate(n) for n in range(N_BRANCH)],
        out_specs=out_specs,
        out_shape=out_shape,
        compiler_params=_params(("parallel", "arbitrary")),
        name="merge",
    )(*ys, *w_brs, h2, h2, h2, h2)
    return outs[0], tuple(outs[1:])


def _outproj_kernel(m_ref, w_ref, x_ref, o_ref, *bf16_refs):
    w = w_ref[...].astype(_BF16)
    if bf16_refs:
        bf16_refs[0][...] = w
    o_ref[...] = x_ref[...] + jnp.dot(m_ref[...], w, preferred_element_type=_F32)


def _outproj(merged, w_out, x2, tm, layer=None, tn=1024):
    m = x2.shape[0]
    w2d = pl.BlockSpec((D_MODEL, tn), lambda i, j: (0, j))
    w_spec = w2d if layer is None else pl.BlockSpec((None, D_MODEL, tn), lambda i, j: (layer, 0, j))
    out_specs = [pl.BlockSpec((tm, tn), lambda i, j: (i, j))]
    out_shape = [jax.ShapeDtypeStruct((m, D_MODEL), _F32)]
    if layer is not None:
        assert m == tm
        out_specs.append(w2d)
        out_shape.append(jax.ShapeDtypeStruct((D_MODEL, D_MODEL), _BF16))
    outs = pl.pallas_call(
        _outproj_kernel,
        grid=(m // tm, D_MODEL // tn),
        in_specs=[pl.BlockSpec((tm, D_MODEL), lambda i, j: (i, 0)), w_spec,
                  pl.BlockSpec((tm, tn), lambda i, j: (i, j))],
        out_specs=out_specs,
        out_shape=out_shape,
        compiler_params=_params(("parallel", "arbitrary")),
        name="outproj",
    )(merged, w_out, x2)
    return outs[0], (outs[1] if layer is not None else None)


def _layer(x3, h2, layer, past, wl, w_br, w_out, kv_bufs):
    b, t_len, _ = x3.shape
    m = b * t_len
    x2 = x3.reshape(m, D_MODEL)
    tm_mat = min(m, 1024)
    h3 = h2.reshape(b, t_len, N_IN)

    if past is None:
        yb, k_buf, v_buf = _attn_prompt(h3, wl["q_norm_g"], wl["k_norm_g"], wl["inv"], layer, kv_bufs)
        states = [jnp.zeros((b, HALO, BR_W), _F32)] * 3
        pos0, tm_br = 0, 256
    else:
        h4 = h2.reshape(b, t_len, N_IN // LANES, LANES)
        yb, k_buf, v_buf = _attn_sample(h4, wl["q_norm_g"], wl["k_norm_g"], wl["inv"], past[0], past[1],
                                        layer, kv_bufs)
        states = [jnp.pad(s[layer], ((0, 0), (HALO - s.shape[2], 0), (0, 0))) for s in past[2:]]
        pos0, tm_br = PAST_LEN, t_len
    yb = yb.reshape(m, BR_W)

    ya, yc, yd, new_a, new_c, new_d = _branches(h3, states, wl, pos0, tm_br)
    ys = (ya.reshape(m, BR_W), yb, yc.reshape(m, BR_W), yd.reshape(m, BR_W))
    w_layer = None if past is None else layer
    merged, w_br_bf16 = _merge(ys, w_br, h2, tm_mat, layer=w_layer)
    out, w_out_bf16 = _outproj(merged, w_out, x2, tm_mat, layer=w_layer)
    return out.reshape(b, t_len, D_MODEL), (k_buf, v_buf), (new_a, new_c, new_d), (w_br_bf16, w_out_bf16)


def kernel(x_prompt, x_sample, cache_attn_k, cache_attn_v, state_conv_a, state_pool_c, state_conv_d,
           norm_g, w_in, q_norm_g, k_norm_g, a_conv_w, c_pool_w, c_scale, d_conv_w, d_conv_b,
           d_ln_g, d_ln_b, w_br_a, w_br_b, w_br_c, w_br_d, w_out):
    half = ROT_DIM // 2
    inv = ROPE_THETA ** (-(jnp.arange(half, dtype=_F32) / half))
    inv = jnp.concatenate([inv, inv, jnp.zeros((LANES - ROT_DIM,), _F32)])[None, :]
    pool_w = c_pool_w.astype(_BF16)
    past = (cache_attn_k, cache_attn_v, state_conv_a, state_pool_c, state_conv_d)

    def on_sublanes(w):
        return jnp.broadcast_to(w[:, None, :], (w.shape[0], 8, w.shape[1]))

    hp, hs = x_prompt, x_sample
    kv_p = kv_s = None
    st_p = [[] for _ in range(3)]
    st_s = [[] for _ in range(3)]
    for l in range(DEPTH):
        wl = {"norm_g": norm_g[l][None], "q_norm_g": q_norm_g[l][None], "k_norm_g": k_norm_g[l][None],
              "inv": inv, "a_conv_w": on_sublanes(a_conv_w[l]), "c_pool_w": pool_w[l], "c_scale": c_scale[l][None],
              "d_conv_w": on_sublanes(d_conv_w[l]), "d_conv_b": d_conv_b[l][None], "d_ln_g": d_ln_g[l][None],
              "d_ln_b": d_ln_b[l][None]}
        h_s, w_in_bf16 = _inproj_cast(hs.reshape(-1, D_MODEL), wl["norm_g"], w_in, l)
        if l == 0:
            h_p, *kv_s = _inproj(hp.reshape(-1, D_MODEL), wl["norm_g"], w_in_bf16, 1024,
                                 caches=(cache_attn_k, cache_attn_v), t_len=x_sample.shape[1])
        else:
            h_p = _inproj(hp.reshape(-1, D_MODEL), wl["norm_g"], w_in_bf16, 1024)
        hs, kv_s, ss, (w_br_bf16, w_out_bf16) = _layer(hs, h_s, l, past, wl, (w_br_a, w_br_b, w_br_c, w_br_d),
                                                       w_out, kv_s)
        hp, kv_p, sp, _ = _layer(hp, h_p, l, None, wl, w_br_bf16, w_out_bf16, kv_p)
        for n in range(3):
            st_p[n].append(sp[n])
            st_s[n].append(ss[n])
    b, seq = x_prompt.shape[:2]
    kv_p = [a.reshape(DEPTH, b, seq, N_HEADS, HEAD_DIM) for a in kv_p]
    kv_s = [a.reshape(cache_attn_k.shape) for a in kv_s]
    return (hp, hs, *kv_p, *(jnp.stack(s, axis=0) for s in st_p),
            *kv_s, *(jnp.stack(s, axis=0) for s in st_s))
```

```python
import functools

import jax
import jax.numpy as jnp
from jax import lax
from jax.experimental import pallas as pl
from jax.experimental.pallas import tpu as pltpu

D_MODEL = 2048
DEPTH = 2
PAST_LEN = 16384
BR_W = D_MODEL // 2
N_BRANCH = 4
A_CONV = 3
HEAD_DIM = 128
N_HEADS = BR_W // HEAD_DIM
GROUPS = ((128, 1), (512, 4), (2048, 16))
N_GROUPS = len(GROUPS)
ROT_DIM = HEAD_DIM // 4
ROPE_THETA = 500000.0
POOL_WINDOWS = (2, 4, 8, 16)
C_GROUP = BR_W // len(POOL_WINDOWS)
POOL_PAST = POOL_WINDOWS[-1] - 1
D_CONV = 31
EPS = 1e-6
N_IN = 4 * BR_W + (N_GROUPS + 3) * BR_W + 2 * BR_W + 3 * BR_W + N_BRANCH * D_MODEL

COL_VA, COL_CA, COL_BA, COL_ZA = 0, 1, 2, 3
COL_Q, COL_K, COL_V, COL_ZB = 4, 7, 8, 9
COL_UC, COL_ZC = 10, 11
COL_GA, COL_GB, COL_ZD = 12, 13, 14
COL_GATE = 15

LANES = 128
QB = 128
SAMPLE_RECENT = 512
HALO = 32
NEG = -1e30
VMEM_LIMIT = 56 * 1024 * 1024

_BF16 = jnp.bfloat16
_F32 = jnp.float32


def _sigmoid(x):
    return 0.5 * jnp.tanh(0.5 * x) + 0.5


def _silu(x):
    return x * _sigmoid(x)


def _params(sem):
    return pltpu.CompilerParams(dimension_semantics=sem, vmem_limit_bytes=VMEM_LIMIT)


def _shift_block(src, nxt, dst, last, t_len):
    nj = src.shape[2]
    keep = 16 - t_len
    dst[0, 0, :, 0:keep] = src[0, 0, :, t_len:16]
    dst[0, 0, 0:nj - 1, keep:16] = src[0, 0, 1:nj, 0:t_len]
    dst[0, 0, nj - 1, keep:16] = jnp.where(last, 0.0, nxt[0, 0, 0, 0:t_len])


def _inproj_kernel(x_ref, g_ref, w_ref, *rest, shift):
    if shift is None:
        h_ref, xn_ref = rest
    else:
        k_ref, kn_ref, v_ref, vn_ref, h_ref, ok_ref, ov_ref, xn_ref = rest
    j = pl.program_id(1)

    @pl.when(j == 0)
    def _():
        x = x_ref[...]
        y = x * lax.rsqrt(jnp.mean(x * x, axis=-1, keepdims=True) + EPS)
        xn_ref[...] = (y * g_ref[...]).astype(_BF16)

    if shift is not None:
        t_len, per_layer, n_steps = shift
        last = lax.rem(jnp.minimum(j, n_steps - 1), per_layer) == per_layer - 1
        _shift_block(k_ref, kn_ref, ok_ref, last, t_len)
        _shift_block(v_ref, vn_ref, ov_ref, last, t_len)

    h_ref[...] = jnp.dot(xn_ref[...], w_ref[...], preferred_element_type=_F32)


def _inproj(x2, g, w_bf16, tm, caches=None, t_len=None, tn=1024, nj_blk=16):
    m = x2.shape[0]
    n_i, n_j = m // tm, N_IN // tn
    in_specs = [
        pl.BlockSpec((tm, D_MODEL), lambda i, j: (i, 0)),
        pl.BlockSpec((1, D_MODEL), lambda i, j: (0, 0)),
        pl.BlockSpec((D_MODEL, tn), lambda i, j: (0, j)),
    ]
    out_specs = [pl.BlockSpec((tm, tn), lambda i, j: (i, j))]
    out_shape = [jax.ShapeDtypeStruct((m, N_IN), _F32)]
    operands = [x2, g, w_bf16]
    shift = None
    if caches is not None:
        depth, b, n_rows = caches[0].shape[:3]
        nj = n_rows // 16
        per_layer = nj // nj_blk
        n_steps = depth * per_layer
        assert b == n_i and n_steps <= n_j and t_len < 16 and nj % nj_blk == 0
        view = (depth, b, nj, 16, N_HEADS, LANES)

        def where(j):
            s = jnp.minimum(j, n_steps - 1)
            return s // per_layer, s % per_layer

        main = pl.BlockSpec((1, 1, nj_blk, 16, N_HEADS, LANES),
                            lambda i, j: (where(j)[0], i, where(j)[1], 0, 0, 0))
        nxt = pl.BlockSpec((1, 1, 1, 16, N_HEADS, LANES),
                           lambda i, j: (where(j)[0], i, jnp.minimum((where(j)[1] + 1) * nj_blk, nj - 1), 0, 0, 0))
        in_specs += [main, nxt, main, nxt]
        out_specs += [main, main]
        out_shape += [jax.ShapeDtypeStruct(view, _F32)] * 2
        kc, vc = caches[0].reshape(view), caches[1].reshape(view)
        operands += [kc, kc, vc, vc]
        shift = (t_len, per_layer, n_steps)
    outs = pl.pallas_call(
        functools.partial(_inproj_kernel, shift=shift),
        grid=(n_i, n_j),
        in_specs=in_specs,
        out_specs=out_specs,
        out_shape=out_shape,
        scratch_shapes=[pltpu.VMEM((tm, D_MODEL), _BF16)],
        compiler_params=_params(("parallel", "arbitrary")),
        name="inproj",
    )(*operands)
    return outs[0] if caches is None else outs


def _inproj_both_kernel(x_ref, xs_ref, g_ref, w_ref, h_ref, hs_ref, xn_ref, xsn_ref):
    i, j = pl.program_id(0), pl.program_id(1)

    def norm(ref):
        x = ref[...]
        return (x * lax.rsqrt(jnp.mean(x * x, axis=-1, keepdims=True) + EPS) * g_ref[...]).astype(_BF16)

    @pl.when(j == 0)
    def _():
        xn_ref[...] = norm(x_ref)

    @pl.when((i == 0) & (j == 0))
    def _():
        xsn_ref[...] = norm(xs_ref)

    wb = w_ref[...].astype(_BF16)
    h_ref[...] = jnp.dot(xn_ref[...], wb, preferred_element_type=_F32)

    @pl.when(i == 0)
    def _():
        hs_ref[...] = jnp.dot(xsn_ref[...], wb, preferred_element_type=_F32)


def _inproj_both(x2, xs2, g, w_in, layer, tm=2048, tn=512):
    m, ms = x2.shape[0], xs2.shape[0]
    n_j = N_IN // tn
    outs = pl.pallas_call(
        _inproj_both_kernel,
        grid=(m // tm, n_j),
        in_specs=[
            pl.BlockSpec((tm, D_MODEL), lambda i, j: (i, 0), pipeline_mode=pl.Buffered(1)),
            pl.BlockSpec((ms, D_MODEL), lambda i, j: (0, 0)),
            pl.BlockSpec((1, D_MODEL), lambda i, j: (0, 0)),
            pl.BlockSpec((None, D_MODEL, tn), lambda i, j: (layer, 0, j)),
        ],
        out_specs=[pl.BlockSpec((tm, tn), lambda i, j: (i, j)),
                   pl.BlockSpec((ms, tn), lambda i, j: (0, jnp.where(i == 0, j, n_j - 1)))],
        out_shape=[jax.ShapeDtypeStruct((m, N_IN), _F32), jax.ShapeDtypeStruct((ms, N_IN), _F32)],
        scratch_shapes=[pltpu.VMEM((tm, D_MODEL), _BF16), pltpu.VMEM((ms, D_MODEL), _BF16)],
        compiler_params=_params(("arbitrary", "arbitrary")),
        name="inproj_both",
    )(x2, xs2, g, w_in)
    return outs


def _inproj_cast_kernel(x_ref, g_ref, w_ref, h_ref, wb_ref, xn_ref):
    @pl.when(pl.program_id(0) == 0)
    def _():
        x = x_ref[...]
        y = x * lax.rsqrt(jnp.mean(x * x, axis=-1, keepdims=True) + EPS)
        xn_ref[...] = (y * g_ref[...]).astype(_BF16)

    wb = w_ref[...].astype(_BF16)
    wb_ref[...] = wb
    h_ref[...] = jnp.dot(xn_ref[...], wb, preferred_element_type=_F32)


def _inproj_cast(x2, g, w_in, layer, tn=1024):
    m = x2.shape[0]
    return pl.pallas_call(
        _inproj_cast_kernel,
        grid=(N_IN // tn,),
        in_specs=[
            pl.BlockSpec((m, D_MODEL), lambda j: (0, 0)),
            pl.BlockSpec((1, D_MODEL), lambda j: (0, 0)),
            pl.BlockSpec((None, D_MODEL, tn), lambda j: (layer, 0, j)),
        ],
        out_specs=[pl.BlockSpec((m, tn), lambda j: (0, j)), pl.BlockSpec((D_MODEL, tn), lambda j: (0, j))],
        out_shape=[jax.ShapeDtypeStruct((m, N_IN), _F32), jax.ShapeDtypeStruct((D_MODEL, N_IN), _BF16)],
        scratch_shapes=[pltpu.VMEM((m, D_MODEL), _BF16)],
        compiler_params=_params(("arbitrary",)),
        name="inproj_cast",
    )(x2, g, w_in)


def _rope_tables(pos, lane, inv):
    ang = pos * inv
    cos_t = jnp.where(lane < ROT_DIM, jnp.cos(ang), 1.0)
    sin_t = jnp.where(lane < ROT_DIM, jnp.sin(ang), 0.0)
    return cos_t, sin_t


def _rotate_half_matrix():
    half = ROT_DIM // 2
    m = lax.broadcasted_iota(jnp.int32, (LANES, LANES), 0)
    l = lax.broadcasted_iota(jnp.int32, (LANES, LANES), 1)
    return jnp.where(m == l + half, jnp.where(l < half, -1.0, 0.0),
                     jnp.where(m == l - half, jnp.where(l < ROT_DIM, 1.0, 0.0), 0.0))


def _rotate_half_lanes(x):
    ax = x.ndim - 1
    half = ROT_DIM // 2
    lane = lax.broadcasted_iota(jnp.int32, x.shape, ax)
    return jnp.where(lane < half, -pltpu.roll(x, LANES - half, ax), pltpu.roll(x, half, ax))


def _dot_hi_lo(x, w2):
    hi = x.astype(_BF16)
    lo = (x - hi.astype(_F32)).astype(_BF16)
    return jnp.dot(jnp.concatenate([hi, lo], axis=1), w2, preferred_element_type=_F32)


def _rms_heads(x, g):
    return x * lax.rsqrt(jnp.mean(x * x, axis=-1, keepdims=True) + EPS) * g


def _attn_block(q, k, v, bias):
    s = lax.dot_general(q.astype(_BF16), k.astype(_BF16), (((1,), (1,)), ((), ())),
                        preferred_element_type=_F32) + bias
    m = jnp.max(s, axis=-1, keepdims=True)
    p = jnp.exp(s - m).astype(_BF16)
    v1 = jnp.concatenate([v.astype(_BF16), jnp.ones(v.shape, _BF16)], axis=1)
    acc = jnp.dot(p, v1, preferred_element_type=_F32)
    l = acc[:, LANES:]
    return acc[:, :LANES] * (1.0 / l), m + jnp.log(l)


def _attn_prompt_kernel(*refs, seq, n_alias):
    q0_ref, q1_ref, q2_ref, k_ref, v_ref, zb_ref, qg_ref, kg_ref, inv_ref = refs[:9]
    yb_ref, ko_ref, vo_ref = refs[9 + n_alias:12 + n_alias]
    cos_ref, sin_ref, band_ref, causal_ref, qs_ref, og_ref, lg_ref = refs[12 + n_alias:]
    rows = 256
    n_chunks = seq // rows

    @pl.when((pl.program_id(0) == 0) & (pl.program_id(1) == 0))
    def _():
        def body(c, carry):
            r0 = pl.multiple_of(c * rows, rows)
            pos = (lax.broadcasted_iota(jnp.int32, (rows, LANES), 0) + r0).astype(_F32)
            lane = lax.broadcasted_iota(jnp.int32, (rows, LANES), 1)
            cos_t, sin_t = _rope_tables(pos, lane, inv_ref[...])
            cos_ref[pl.ds(r0, rows), :] = cos_t
            sin_ref[pl.ds(r0, rows), :] = sin_t
            return carry
        lax.fori_loop(0, n_chunks, body, 0)
        ri = lax.broadcasted_iota(jnp.int32, (QB, 2 * QB), 0)
        ci = lax.broadcasted_iota(jnp.int32, (QB, 2 * QB), 1)
        band_ref[...] = jnp.where(ci < ri, NEG, jnp.where(ci > ri + QB, NEG, 0.0))
        rq = lax.broadcasted_iota(jnp.int32, (QB, QB), 0)
        cq = lax.broadcasted_iota(jnp.int32, (QB, QB), 1)
        causal_ref[...] = jnp.where(cq > rq, NEG, 0.0)

    scale = HEAD_DIM ** -0.5
    rot_m = _rotate_half_matrix().astype(_BF16)
    rot2 = jnp.concatenate([rot_m, rot_m], axis=0)
    ones2 = jnp.ones((2 * LANES, LANES), _BF16)

    def norm_rope(x, g, cos_t, sin_t):
        ssq = _dot_hi_lo(x * x, ones2)
        y = x * lax.rsqrt(ssq * (1.0 / HEAD_DIM) + EPS) * g
        return y * cos_t + _dot_hi_lo(y, rot2) * sin_t

    def prep(c, carry):
        r0 = pl.multiple_of(c * rows, rows)
        sl = pl.ds(r0, rows)
        cos_t, sin_t = cos_ref[sl, :], sin_ref[sl, :]
        for g, q_ref in enumerate((q0_ref, q1_ref, q2_ref)):
            qs_ref[g, sl, :] = norm_rope(q_ref[0, sl, :], qg_ref[...] * scale, cos_t, sin_t)
        ko_ref[0, 0, sl, :] = norm_rope(k_ref[0, sl, :], kg_ref[...], cos_t, sin_t)
        vo_ref[0, 0, sl, :] = v_ref[0, sl, :]
        for l2 in range(1, ko_ref.shape[0]):
            ko_ref[l2, 0, sl, :] = jnp.zeros((rows, LANES), _F32)
            vo_ref[l2, 0, sl, :] = jnp.zeros((rows, LANES), _F32)
        return carry
    lax.fori_loop(0, n_chunks, prep, 0, unroll=True)

    def rows_of(start, n, d):
        return pl.ds(start, n) if d == 1 else pl.ds(start, n, stride=d)

    for g, (w, d) in enumerate(GROUPS):
        assert w // d == QB
        n_blk = seq // (d * QB)
        for r in range(d):
            for blk in range(n_blk):
                q = qs_ref[g, rows_of(r + d * QB * blk, QB, d), :]
                if blk == 0:
                    ksl, bias = rows_of(r, QB, d), causal_ref[...]
                else:
                    ksl, bias = rows_of(r + d * QB * (blk - 1), 2 * QB, d), band_ref[...]
                o, lse = _attn_block(q, ko_ref[0, 0, ksl, :], v_ref[0, ksl, :], bias)
                osl = rows_of(r + d * QB * blk, QB, d)
                og_ref[g, osl, :] = o
                lg_ref[g, osl, :] = lse

    def merge(c, carry):
        r0 = pl.multiple_of(c * rows, rows)
        sl = pl.ds(r0, rows)
        l0, l1, l2 = lg_ref[0, sl, :], lg_ref[1, sl, :], lg_ref[2, sl, :]
        mx = jnp.maximum(jnp.maximum(l0, l1), l2)
        w0, w1, w2 = jnp.exp(l0 - mx), jnp.exp(l1 - mx), jnp.exp(l2 - mx)
        o = (w0 * og_ref[0, sl, :] + w1 * og_ref[1, sl, :] + w2 * og_ref[2, sl, :]) / (w0 + w1 + w2)
        yb_ref[0, sl, :] = (o * _silu(zb_ref[0, sl, :])).astype(_BF16)
        return carry
    lax.fori_loop(0, n_chunks, merge, 0)


def _attn_prompt(h3, qg, kg, inv, layer, kv_bufs):
    b, seq, _ = h3.shape
    hb = BR_W // LANES
    first = kv_bufs is None
    assert first == (layer == 0)

    def col(cb):
        return pl.BlockSpec((1, seq, LANES), lambda i, j: (i, 0, cb * hb + j))

    small = pl.BlockSpec((1, LANES), lambda i, j: (0, 0))
    any_spec = pl.BlockSpec(memory_space=pl.ANY)
    kv_out = pl.BlockSpec((DEPTH if first else 1, 1, seq, LANES), lambda i, j: (layer, i, 0, j))
    kv_shape = jax.ShapeDtypeStruct((DEPTH, b, seq, BR_W), _F32)
    n_alias = 0 if first else 2
    return pl.pallas_call(
        functools.partial(_attn_prompt_kernel, seq=seq, n_alias=n_alias),
        grid=(b, N_HEADS),
        in_specs=[col(COL_Q), col(COL_Q + 1), col(COL_Q + 2), col(COL_K), col(COL_V), col(COL_ZB),
                  small, small, small] + [any_spec] * n_alias,
        out_specs=[pl.BlockSpec((1, seq, LANES), lambda i, j: (i, 0, j)), kv_out, kv_out],
        out_shape=[jax.ShapeDtypeStruct((b, seq, BR_W), _BF16), kv_shape, kv_shape],
        input_output_aliases={} if first else {9: 1, 10: 2},
        scratch_shapes=[pltpu.VMEM((seq, LANES), _F32)] * 2
        + [pltpu.VMEM((QB, 2 * QB), _F32), pltpu.VMEM((QB, QB), _F32)]
        + [pltpu.VMEM((N_GROUPS, seq, LANES), _F32)] * 3,
        compiler_params=_params(("arbitrary", "arbitrary")),
        name="attn_prompt",
    )(h3, h3, h3, h3, h3, h3, qg, kg, inv, *(() if first else kv_bufs))


def _attn_sample_kernel(q0_ref, q1_ref, q2_ref, k_ref, v_ref, zb_ref, qg_ref, kg_ref, inv_ref,
                        kres_ref, krec_ref, vres_ref, vrec_ref, kbuf_hbm, vbuf_hbm,
                        yb_ref, ok_hbm, ov_hbm, kn_ref, vn_ref, sem, *, t_len, n_rows, layer):
    b = pl.program_id(0)
    shape = (t_len, N_HEADS, LANES)
    pos = (lax.broadcasted_iota(jnp.int32, shape, 0) + PAST_LEN).astype(_F32)
    lane = lax.broadcasted_iota(jnp.int32, shape, 2)
    cos_t, sin_t = _rope_tables(pos, lane, inv_ref[...].reshape(1, 1, LANES))

    def norm_rope(x, g_ref):
        y = _rms_heads(x, g_ref[...].reshape(1, 1, LANES))
        return y * cos_t + _rotate_half_lanes(y) * sin_t

    scale = HEAD_DIM ** -0.5
    k_new = norm_rope(k_ref[0], kg_ref)
    v_new = v_ref[0]
    kn_ref[...] = k_new
    vn_ref[...] = v_new
    nj = n_rows // 16
    tail = pl.ds(16 - t_len, t_len)
    copies = [pltpu.make_async_copy(kn_ref, ok_hbm.at[layer, b, nj - 1, tail], sem.at[0]),
              pltpu.make_async_copy(vn_ref, ov_hbm.at[layer, b, nj - 1, tail], sem.at[1])]
    for cp in copies:
        cp.start()

    qs = [norm_rope(q_ref[0], qg_ref) * scale for q_ref in (q0_ref, q1_ref, q2_ref)]

    for t in range(t_len):
        outs, lses = [], []
        for g, (w, d) in enumerate(GROUPS):
            q = qs[g][t]
            pieces = []
            nrec = krec_ref.shape[2]
            if d == 1:
                lo = nrec - QB // 16
                kc = krec_ref[0, 0, lo:nrec].reshape(QB, N_HEADS, LANES)
                vc = vrec_ref[0, 0, lo:nrec].reshape(QB, N_HEADS, LANES)
                idx = lax.broadcasted_iota(jnp.int32, (QB, N_HEADS, 1), 0)
                pieces.append((kc, vc, idx >= t))
                pieces.append((k_new[:t + 1], v_new[:t + 1], None))
            elif d < 16:
                per = 16 // d
                lo = nrec - (w // d) // per
                for m in range(per):
                    pieces.append((krec_ref[0, 0, lo:nrec, t + d * m], vrec_ref[0, 0, lo:nrec, t + d * m], None))
                pieces.append((k_new[t:t + 1], v_new[t:t + 1], None))
            else:
                assert d == 16
                lo = nj - w // 16
                pieces.append((kres_ref[0, 0, lo:nj, t], vres_ref[0, 0, lo:nj, t], None))
                pieces.append((k_new[t:t + 1], v_new[t:t + 1], None))
            scores = []
            for kk, vv, mask in pieces:
                s = jnp.sum(kk * q[None], axis=-1, keepdims=True)
                if mask is not None:
                    s = jnp.where(mask, s, NEG)
                scores.append(s)
            mx = functools.reduce(jnp.maximum, [jnp.max(s, axis=0) for s in scores])
            den = 0.0
            acc = 0.0
            for s, (kk, vv, mask) in zip(scores, pieces):
                p = jnp.exp(s - mx[None])
                den = den + jnp.sum(p, axis=0)
                acc = acc + jnp.sum(p * vv, axis=0)
            outs.append(acc / den)
            lses.append(mx + jnp.log(den))
        mx = jnp.maximum(jnp.maximum(lses[0], lses[1]), lses[2])
        ws = [jnp.exp(l - mx) for l in lses]
        o = (ws[0] * outs[0] + ws[1] * outs[1] + ws[2] * outs[2]) / (ws[0] + ws[1] + ws[2])
        yb_ref[0, t] = o * _silu(zb_ref[0, t])

    for cp in copies:
        cp.wait()


def _attn_sample(h4, qg, kg, inv, cache_k, cache_v, layer, kv_bufs):
    b, t_len = h4.shape[:2]
    n_rows = cache_k.shape[2]
    kc = cache_k.reshape(DEPTH, b, n_rows // 16, 16, N_HEADS, LANES)
    vc = cache_v.reshape(DEPTH, b, n_rows // 16, 16, N_HEADS, LANES)

    def col(cb):
        return pl.BlockSpec((1, t_len, N_HEADS, LANES), lambda i: (i, 0, cb, 0))

    small = pl.BlockSpec((1, LANES), lambda i: (0, 0))
    nj = n_rows // 16
    nrec = SAMPLE_RECENT // 16
    assert nj % nrec == 0 and 16 % t_len == 0 and all(w <= SAMPLE_RECENT for w, d in GROUPS if d < 16)
    resid = pl.BlockSpec((1, 1, nj, t_len, N_HEADS, LANES), lambda i: (layer, i, 0, 0, 0, 0))
    recent = pl.BlockSpec((1, 1, nrec, 16, N_HEADS, LANES), lambda i: (layer, i, nj // nrec - 1, 0, 0, 0))
    any_spec = pl.BlockSpec(memory_space=pl.ANY)
    rows_shape = (t_len, N_HEADS, LANES)
    return pl.pallas_call(
        functools.partial(_attn_sample_kernel, t_len=t_len, n_rows=n_rows, layer=layer),
        grid=(b,),
        in_specs=[col(COL_Q), col(COL_Q + 1), col(COL_Q + 2), col(COL_K), col(COL_V), col(COL_ZB),
                  small, small, small, resid, recent, resid, recent, any_spec, any_spec],
        out_specs=[pl.BlockSpec((1, t_len, N_HEADS, LANES), lambda i: (i, 0, 0, 0)), any_spec, any_spec],
        out_shape=[jax.ShapeDtypeStruct((b, t_len, N_HEADS, LANES), _F32),
                   jax.ShapeDtypeStruct(kc.shape, _F32), jax.ShapeDtypeStruct(vc.shape, _F32)],
        input_output_aliases={13: 1, 14: 2},
        scratch_shapes=[pltpu.VMEM(rows_shape, _F32)] * 2 + [pltpu.SemaphoreType.DMA((2,))],
        compiler_params=_params(("arbitrary",)),
        name="attn_sample",
    )(h4, h4, h4, h4, h4, h4, qg, kg, inv, kc, kc, vc, vc, *kv_bufs)


def _branch_kernel(va_ref, ca_ref, ba_ref, za_ref, uc_ref, zc_ref, ga_ref, gb_ref, zd_ref,
                   sta_ref, stc_ref, std_ref, aw_ref, pw_ref, cs_ref, dw_ref, db_ref, lg_ref, lb_ref,
                   ya_ref, yc_ref, yd_ref, na_ref, nc_ref, nd_ref,
                   eas_ref, ec_ref, eds_ref, pp_ref, *, tm, pos0):
    i = pl.program_id(1)
    ea_ref = eas_ref.at[0]
    ed_ref = eds_ref.at[0]
    a_offs = [HALO - (A_CONV - 1) + k for k in range(A_CONV)]
    a_shifts = sorted({off % 8 for off in a_offs} | {0})
    assert len(a_shifts) <= eas_ref.shape[0]

    @pl.when(i == 0)
    def _():
        ea_ref[0:HALO, :] = sta_ref[0]
        ec_ref[0:HALO, :] = stc_ref[0]
        ed_ref[0:HALO, :] = std_ref[0]
        pp_ref[...] = jnp.zeros_like(pp_ref)

    @pl.when(i > 0)
    def _():
        for e_ref in (ea_ref, ec_ref, ed_ref):
            e_ref[0:HALO, :] = e_ref[tm:tm + HALO, :]

    new = pl.ds(HALO, tm)
    ea_ref[new, :] = ca_ref[0] * va_ref[0]
    ec_ref[new, :] = uc_ref[0]
    ed_ref[new, :] = ga_ref[0] * _sigmoid(gb_ref[0])
    n_shift = HALO + tm - 8
    for r in range(1, 8):
        eds_ref[r, 0:n_shift, :] = ed_ref[pl.ds(r, n_shift), :]
    for slot, r in enumerate(a_shifts):
        if r:
            eas_ref[slot, 0:n_shift, :] = ea_ref[pl.ds(r, n_shift), :]

    pos = pos0 + i * tm + lax.broadcasted_iota(jnp.int32, (tm, C_GROUP), 0)
    for g, w in enumerate(POOL_WINDOWS):
        cols = slice(g * C_GROUP, (g + 1) * C_GROUP)
        tok = ec_ref[new, cols]
        s = tok
        for j in range(1, w):
            s = s + ec_ref[pl.ds(HALO - j, tm), cols]
        cnt = jnp.minimum(pos + 1, w).astype(_F32)
        pp_ref[0:tm, :] = s / cnt - tok
        y = jnp.dot(pp_ref[...].astype(_BF16), pw_ref[g], preferred_element_type=_F32)[0:tm]
        yc_ref[0, :, cols] = ((y * cs_ref[:, cols]) * _silu(zc_ref[0, :, cols])).astype(_BF16)

    rc = min(tm, 32)

    def conv_rows(w_ref, e_ref, slot_of, offs, r0):
        sub = min(rc, 8)
        accs = [None] * (rc // sub)
        for k, off in enumerate(offs):
            w = w_ref[k, 0:sub, :]
            for n in range(rc // sub):
                term = w * e_ref[slot_of(off % 8), pl.ds(off // 8 * 8 + r0 + sub * n, sub), :]
                accs[n] = term if accs[n] is None else accs[n] + term
        return accs[0] if len(accs) == 1 else jnp.concatenate(accs, axis=0)

    def chunk(r0):
        rows = pl.ds(r0, rc)
        conv = conv_rows(aw_ref, eas_ref, a_shifts.index, a_offs, r0)
        ya_ref[0, rows, :] = (ba_ref[0, rows, :] * conv * _silu(za_ref[0, rows, :])).astype(_BF16)
        d_offs = [HALO - (D_CONV - 1) + k for k in range(D_CONV)]
        x = conv_rows(dw_ref, eds_ref, lambda r: r, d_offs, r0) + db_ref[...]
        mu = jnp.mean(x, axis=-1, keepdims=True)
        xc = x - mu
        var = jnp.mean(xc * xc, axis=-1, keepdims=True)
        y = xc * lax.rsqrt(var + EPS) * lg_ref[...] + lb_ref[...]
        yd_ref[0, rows, :] = (_silu(y) * _silu(zd_ref[0, rows, :])).astype(_BF16)

    if tm == rc:
        chunk(0)
    else:
        def body(c, carry):
            chunk(pl.multiple_of(c * rc, rc))
            return carry
        lax.fori_loop(0, tm // rc, body, 0, unroll=2)

    @pl.when(i == pl.num_programs(1) - 1)
    def _():
        na_ref[0] = ea_ref[pl.ds(HALO + tm - (A_CONV - 1), A_CONV - 1), :]
        nc_ref[0] = ec_ref[pl.ds(HALO + tm - POOL_PAST, POOL_PAST), :]
        nd_ref[0] = ed_ref[pl.ds(HALO + tm - (D_CONV - 1), D_CONV - 1), :]


def _branches(h3, states, wl, pos0, tm):
    b, t_len, _ = h3.shape

    def col(cb):
        return pl.BlockSpec((1, tm, BR_W), lambda i, j: (i, j, cb))

    def full(shape):
        return pl.BlockSpec(shape, lambda i, j: (0,) * len(shape))

    state = pl.BlockSpec((1, HALO, BR_W), lambda i, j: (i, 0, 0))
    y_out = pl.BlockSpec((1, tm, BR_W), lambda i, j: (i, j, 0))

    def st_out(n):
        return pl.BlockSpec((1, n, BR_W), lambda i, j: (i, 0, 0))

    tmc = max(tm, 8)
    n_pool = len(POOL_WINDOWS)
    return pl.pallas_call(
        functools.partial(_branch_kernel, tm=tm, pos0=pos0),
        grid=(b, t_len // tm),
        in_specs=[col(COL_VA), col(COL_CA), col(COL_BA), col(COL_ZA), col(COL_UC), col(COL_ZC),
                  col(COL_GA), col(COL_GB), col(COL_ZD), state, state, state,
                  full((A_CONV, 8, BR_W)), full((n_pool, C_GROUP, C_GROUP)), full((1, BR_W)),
                  full((D_CONV, 8, BR_W)), full((1, BR_W)), full((1, BR_W)), full((1, BR_W))],
        out_specs=[y_out, y_out, y_out, st_out(A_CONV - 1), st_out(POOL_PAST), st_out(D_CONV - 1)],
        out_shape=[jax.ShapeDtypeStruct((b, t_len, BR_W), _BF16)] * 3
        + [jax.ShapeDtypeStruct((b, n, BR_W), _F32) for n in (A_CONV - 1, POOL_PAST, D_CONV - 1)],
        scratch_shapes=[pltpu.VMEM((A_CONV, HALO + tmc, BR_W), _F32), pltpu.VMEM((HALO + tmc, BR_W), _F32),
                        pltpu.VMEM((8, HALO + tmc, BR_W), _F32), pltpu.VMEM((tmc, C_GROUP), _F32)],
        compiler_params=_params(("arbitrary", "arbitrary")),
        name="branches",
    )(*([h3] * 9), *states, wl["a_conv_w"], wl["c_pool_w"], wl["c_scale"], wl["d_conv_w"],
      wl["d_conv_b"], wl["d_ln_g"], wl["d_ln_b"])


def _merge_kernel(ya_ref, yb_ref, yc_ref, yd_ref, wa_ref, wb_ref, wc_ref, wd_ref,
                  g0_ref, g1_ref, g2_ref, g3_ref, o_ref, *bf16_refs):
    acc = None
    for n, (y_ref, w_ref, g_ref) in enumerate(((ya_ref, wa_ref, g0_ref), (yb_ref, wb_ref, g1_ref),
                                               (yc_ref, wc_ref, g2_ref), (yd_ref, wd_ref, g3_ref))):
        w = w_ref[...].astype(_BF16)
        if bf16_refs:
            bf16_refs[n][...] = w
        term = _sigmoid(g_ref[...]) * jnp.dot(y_ref[...].astype(_BF16), w, preferred_element_type=_F32)
        acc = term if acc is None else acc + term
    o_ref[...] = acc.astype(_BF16)


def _merge(ys, w_brs, h2, tm, layer=None, tn=512):
    m = h2.shape[0]
    gate0 = COL_GATE * BR_W // tn
    y_spec = pl.BlockSpec((tm, BR_W), lambda i, j: (i, 0))
    w2d = pl.BlockSpec((BR_W, tn), lambda i, j: (0, j))
    w_spec = w2d if layer is None else pl.BlockSpec((None, BR_W, tn), lambda i, j: (layer, 0, j))
    out_specs = [pl.BlockSpec((tm, tn), lambda i, j: (i, j))]
    out_shape = [jax.ShapeDtypeStruct((m, D_MODEL), _BF16)]
    if layer is not None:
        assert m == tm
        out_specs += [w2d] * N_BRANCH
        out_shape += [jax.ShapeDtypeStruct((BR_W, D_MODEL), _BF16)] * N_BRANCH

    def gate(n):
        return pl.BlockSpec((tm, tn), lambda i, j: (i, gate0 + n * (D_MODEL // tn) + j))

    outs = pl.pallas_call(
        _merge_kernel,
        grid=(m // tm, D_MODEL // tn),
        in_specs=[y_spec] * 4 + [w_spec] * 4 + [gate(n) for n in range(N_BRANCH)],
        out_specs=out_specs,
        out_shape=out_shape,
        compiler_params=_params(("parallel", "arbitrary")),
        name="merge",
    )(*ys, *w_brs, h2, h2, h2, h2)
    return outs[0], tuple(outs[1:])


def _outproj_kernel(m_ref, w_ref, x_ref, o_ref, *bf16_refs):
    w = w_ref[...].astype(_BF16)
    if bf16_refs:
        bf16_refs[0][...] = w
    o_ref[...] = x_ref[...] + jnp.dot(m_ref[...], w, preferred_element_type=_F32)


def _outproj(merged, w_out, x2, tm, layer=None, tn=1024):
    m = x2.shape[0]
    w2d = pl.BlockSpec((D_MODEL, tn), lambda i, j: (0, j))
    w_spec = w2d if layer is None else pl.BlockSpec((None, D_MODEL, tn), lambda i, j: (layer, 0, j))
    out_specs = [pl.BlockSpec((tm, tn), lambda i, j: (i, j))]
    out_shape = [jax.ShapeDtypeStruct((m, D_MODEL), _F32)]
    if layer is not None:
        assert m == tm
        out_specs.append(w2d)
        out_shape.append(jax.ShapeDtypeStruct((D_MODEL, D_MODEL), _BF16))
    outs = pl.pallas_call(
        _outproj_kernel,
        grid=(m // tm, D_MODEL // tn),
        in_specs=[pl.BlockSpec((tm, D_MODEL), lambda i, j: (i, 0)), w_spec,
                  pl.BlockSpec((tm, tn), lambda i, j: (i, j))],
        out_specs=out_specs,
        out_shape=out_shape,
        compiler_params=_params(("parallel", "arbitrary")),
        name="outproj",
    )(merged, w_out, x2)
    return outs[0], (outs[1] if layer is not None else None)


def _layer(x3, h2, layer, past, wl, w_br, w_out, kv_bufs):
    b, t_len, _ = x3.shape
    m = b * t_len
    x2 = x3.reshape(m, D_MODEL)
    tm_mat = min(m, 1024)
    h3 = h2.reshape(b, t_len, N_IN)

    if past is None:
        yb, k_buf, v_buf = _attn_prompt(h3, wl["q_norm_g"], wl["k_norm_g"], wl["inv"], layer, kv_bufs)
        states = [jnp.zeros((b, HALO, BR_W), _F32)] * 3
        pos0, tm_br = 0, 256
    else:
        h4 = h2.reshape(b, t_len, N_IN // LANES, LANES)
        yb, k_buf, v_buf = _attn_sample(h4, wl["q_norm_g"], wl["k_norm_g"], wl["inv"], past[0], past[1],
                                        layer, kv_bufs)
        states = [jnp.pad(s[layer], ((0, 0), (HALO - s.shape[2], 0), (0, 0))) for s in past[2:]]
        pos0, tm_br = PAST_LEN, t_len
    yb = yb.reshape(m, BR_W)

    ya, yc, yd, new_a, new_c, new_d = _branches(h3, states, wl, pos0, tm_br)
    ys = (ya.reshape(m, BR_W), yb, yc.reshape(m, BR_W), yd.reshape(m, BR_W))
    w_layer = None if past is None else layer
    merged, w_br_bf16 = _merge(ys, w_br, h2, tm_mat, layer=w_layer)
    out, w_out_bf16 = _outproj(merged, w_out, x2, tm_mat, layer=w_layer)
    return out.reshape(b, t_len, D_MODEL), (k_buf, v_buf), (new_a, new_c, new_d), (w_br_bf16, w_out_bf16)


def kernel(x_prompt, x_sample, cache_attn_k, cache_attn_v, state_conv_a, state_pool_c, state_conv_d,
           norm_g, w_in, q_norm_g, k_norm_g, a_conv_w, c_pool_w, c_scale, d_conv_w, d_conv_b,
           d_ln_g, d_ln_b, w_br_a, w_br_b, w_br_c, w_br_d, w_out):
    half = ROT_DIM // 2
    inv = ROPE_THETA ** (-(jnp.arange(half, dtype=_F32) / half))
    inv = jnp.concatenate([inv, inv, jnp.zeros((LANES - ROT_DIM,), _F32)])[None, :]
    pool_w = c_pool_w.astype(_BF16)
    past = (cache_attn_k, cache_attn_v, state_conv_a, state_pool_c, state_conv_d)

    def on_sublanes(w):
        return jnp.broadcast_to(w[:, None, :], (w.shape[0], 8, w.shape[1]))

    hp, hs = x_prompt, x_sample
    kv_p = kv_s = None
    st_p = [[] for _ in range(3)]
    st_s = [[] for _ in range(3)]
    for l in range(DEPTH):
        wl = {"norm_g": norm_g[l][None], "q_norm_g": q_norm_g[l][None], "k_norm_g": k_norm_g[l][None],
              "inv": inv, "a_conv_w": on_sublanes(a_conv_w[l]), "c_pool_w": pool_w[l], "c_scale": c_scale[l][None],
              "d_conv_w": on_sublanes(d_conv_w[l]), "d_conv_b": d_conv_b[l][None], "d_ln_g": d_ln_g[l][None],
              "d_ln_b": d_ln_b[l][None]}
        if l == 0:
            h_s, w_in_bf16 = _inproj_cast(hs.reshape(-1, D_MODEL), wl["norm_g"], w_in, l)
            h_p, *kv_s = _inproj(hp.reshape(-1, D_MODEL), wl["norm_g"], w_in_bf16, 1024,
                                 caches=(cache_attn_k, cache_attn_v), t_len=x_sample.shape[1])
        else:
            h_p, h_s = _inproj_both(hp.reshape(-1, D_MODEL), hs.reshape(-1, D_MODEL), wl["norm_g"], w_in, l)
        hs, kv_s, ss, (w_br_bf16, w_out_bf16) = _layer(hs, h_s, l, past, wl, (w_br_a, w_br_b, w_br_c, w_br_d),
                                                       w_out, kv_s)
        hp, kv_p, sp, _ = _layer(hp, h_p, l, None, wl, w_br_bf16, w_out_bf16, kv_p)
        for n in range(3):
            st_p[n].append(sp[n])
            st_s[n].append(ss[n])
    b, seq = x_prompt.shape[:2]
    kv_p = [a.reshape(DEPTH, b, seq, N_HEADS, HEAD_DIM) for a in kv_p]
    kv_s = [a.reshape(cache_attn_k.shape) for a in kv_s]
    return (hp, hs, *kv_p, *(jnp.stack(s, axis=0) for s in st_p),
            *kv_s, *(jnp.stack(s, axis=0) for s in st_s))
```
